```python
import math
import jax, jax.numpy as jnp
from jax import lax
import numpy as np

D_MODEL = 1024
BATCH = 4
SEQ = 4096
DEPTH = 2

D_MIX = D_MODEL
A_HEADS = 4
A_HEAD_DIM = 128
A_WIDTH = A_HEADS * A_HEAD_DIM
CONV_K = 4
CHUNK = 64
B_Q_HEADS = 8
B_KV_HEADS = 2
B_HEAD_DIM = 64
B_GROUP = B_Q_HEADS // B_KV_HEADS
B_WIDTH = B_Q_HEADS * B_HEAD_DIM
B_KV_WIDTH = B_KV_HEADS * B_HEAD_DIM
WINDOW = 128
BLOCK = 128
IN_SIZES = (A_WIDTH, A_WIDTH, A_WIDTH, A_WIDTH, A_HEADS, A_HEADS,
            B_WIDTH, B_KV_WIDTH, B_KV_WIDTH, B_WIDTH)
IN_COLS = sum(IN_SIZES)
DEEPNORM_ALPHA = (2 * DEPTH) ** 0.25
DEEPNORM_BETA = (8 * DEPTH) ** -0.25
LN_EPS = 1e-5
RMS_EPS = 1e-6
L2_EPS = 1e-6

kernel_name = "hybrid_deltanet_swa_sink_alibi_deepnorm"


def _alibi_slopes(n_heads):
    return jnp.asarray([2.0 ** (-8.0 * (h + 1) / n_heads) for h in range(n_heads)], dtype=jnp.float32)


def _layernorm(x, g, b):
    xf = x.astype(jnp.float32)
    mu = jnp.mean(xf, axis=-1, keepdims=True)
    var = jnp.mean(jnp.square(xf - mu), axis=-1, keepdims=True)
    y = (xf - mu) * lax.rsqrt(var + LN_EPS) * g.astype(jnp.float32) + b.astype(jnp.float32)
    return y.astype(x.dtype)


def _l2norm(t):
    return t * lax.rsqrt(jnp.sum(jnp.square(t), axis=-1, keepdims=True) + L2_EPS)


def _short_conv(x, w):
    c = x.shape[-1]
    return lax.conv_general_dilated(
        x, w[:, None, :].astype(x.dtype), window_strides=(1,), padding=[(CONV_K - 1, 0)],
        dimension_numbers=('NWC', 'WIO', 'NWC'), feature_group_count=c)


def _gated_delta_rule(q, k, v, g, beta):
    bsz, t_len, h, dk = q.shape
    dv = v.shape[-1]
    n = t_len // CHUNK

    def chunks(t):
        t = t.reshape((bsz, n, CHUNK, h) + t.shape[3:])
        return jnp.moveaxis(t, 3, 1)

    q, k, v, g, beta = chunks(q), chunks(k), chunks(v), chunks(g), chunks(beta)
    g = jnp.cumsum(g, axis=-1)
    causal = jnp.tril(jnp.ones((CHUNK, CHUNK), dtype=bool))
    strict = jnp.tril(jnp.ones((CHUNK, CHUNK), dtype=bool), -1)
    decay = jnp.exp(jnp.where(causal, g[..., :, None] - g[..., None, :], -jnp.inf))
    k_beta = k * beta[..., None]
    a_mat = jnp.where(strict, jnp.einsum('bhncd,bhnsd->bhncs', k_beta, k) * decay, 0.0)
    eye = jnp.eye(CHUNK, dtype=a_mat.dtype)
    t_mat = lax.linalg.triangular_solve(eye + a_mat, jnp.broadcast_to(eye, a_mat.shape),
                                        left_side=True, lower=True)
    u = jnp.einsum('bhncs,bhnse->bhnce', t_mat, v * beta[..., None])
    w = jnp.einsum('bhncs,bhnsd->bhncd', t_mat, k_beta * jnp.exp(g)[..., None])
    qk = jnp.where(causal, jnp.einsum('bhncd,bhnsd->bhncs', q, k) * decay, 0.0)
    q_dec = q * jnp.exp(g)[..., None]
    k_dec = k * jnp.exp(g[..., -1:] - g)[..., None]
    g_tot = jnp.exp(g[..., -1])
    xs = tuple(jnp.moveaxis(t, 2, 0) for t in (q_dec, k_dec, u, w, qk, g_tot))

    def step(s, inp):
        q_c, k_c, u_c, w_c, qk_c, gt = inp
        v_new = u_c - jnp.einsum('bhcd,bhde->bhce', w_c, s)
        o = jnp.einsum('bhcd,bhde->bhce', q_c, s) + jnp.einsum('bhcs,bhse->bhce', qk_c, v_new)
        s = s * gt[..., None, None] + jnp.einsum('bhcd,bhce->bhde', k_c, v_new)
        return s, o

    s0 = jnp.zeros((bsz, h, dk, dv), jnp.float32)
    _, o = lax.scan(step, s0, xs)
    o = jnp.moveaxis(o, 0, 2)
    return jnp.moveaxis(o, 1, 3).reshape(bsz, t_len, h, dv)


def _deltanet_group(q, k, v, z, b, a, conv_w, a_log, dt_bias, norm_w):
    bsz, t_len, _ = q.shape
    qkv = jax.nn.silu(_short_conv(jnp.concatenate([q, k, v], axis=-1), conv_w))
    q, k, v = jnp.split(qkv, 3, axis=-1)
    heads = lambda t: t.reshape(bsz, t_len, A_HEADS, A_HEAD_DIM).astype(jnp.float32)
    q = _l2norm(heads(q)) * (A_HEAD_DIM ** -0.5)
    k = _l2norm(heads(k))
    v = heads(v)
    beta = jax.nn.sigmoid(b.astype(jnp.float32))
    g = -jnp.exp(a_log.astype(jnp.float32)) * jax.nn.softplus(
        a.astype(jnp.float32) + dt_bias.astype(jnp.float32))
    o = _gated_delta_rule(q, k, v, g, beta)
    o = o * lax.rsqrt(jnp.mean(jnp.square(o), axis=-1, keepdims=True) + RMS_EPS) * norm_w.astype(jnp.float32)
    return o.reshape(bsz, t_len, A_WIDTH).astype(z.dtype) * jax.nn.silu(z)


def _swa_group(q, k, v, z, sinks):
    bsz, t_len, _ = q.shape
    n = t_len // BLOCK
    q = q.reshape(bsz, n, BLOCK, B_KV_HEADS, B_GROUP, B_HEAD_DIM)
    k = k.reshape(bsz, t_len, B_KV_HEADS, B_HEAD_DIM)
    v = v.reshape(bsz, t_len, B_KV_HEADS, B_HEAD_DIM)

    def band(t):
        prev = jnp.pad(t, ((0, 0), (BLOCK, 0), (0, 0), (0, 0)))[:, :t_len]
        shp = (bsz, n, BLOCK, B_KV_HEADS, B_HEAD_DIM)
        return jnp.concatenate([prev.reshape(shp), t.reshape(shp)], axis=2)

    kb, vb = band(k), band(v)
    s = jnp.einsum('bnqhgd,bnshd->bhgnqs', q, kb).astype(jnp.float32) * (B_HEAD_DIM ** -0.5)
    q_idx = jnp.arange(BLOCK)[:, None]
    s_idx = jnp.arange(2 * BLOCK)[None, :]
    dist = q_idx + BLOCK - s_idx
    key_pos = jnp.arange(n)[:, None] * BLOCK - BLOCK + jnp.arange(2 * BLOCK)[None, :]
    mask = ((dist >= 0) & (dist < WINDOW))[None] & (key_pos >= 0)[:, None, :]
    slopes = _alibi_slopes(B_Q_HEADS).reshape(B_KV_HEADS, B_GROUP)
    s = s - slopes[:, :, None, None, None] * dist.astype(jnp.float32)
    s = jnp.where(mask, s, -jnp.inf)
    sink = sinks.astype(jnp.float32).reshape(B_KV_HEADS, B_GROUP)[:, :, None, None]
    m = jnp.maximum(jnp.max(s, axis=-1), sink)
    p = jnp.exp(s - m[..., None])
    p = p / (jnp.sum(p, axis=-1, keepdims=True) + jnp.exp(sink - m)[..., None])
    o = jnp.einsum('bhgnqs,bnshd->bnqhgd', p.astype(vb.dtype), vb)
    return o.reshape(bsz, t_len, B_WIDTH) * jax.nn.silu(z)


def _layer(x, w_in, conv_w, a_log, dt_bias, norm_w, sinks, w_out, ln_g, ln_b):
    h = jnp.einsum('btd,dc->btc', x, w_in)
    offsets = [sum(IN_SIZES[:i]) for i in range(1, len(IN_SIZES))]
    qa, ka, va, za, ba, aa, qb, kb, vb, zb = jnp.split(h, offsets, axis=-1)
    ya = _deltanet_group(qa, ka, va, za, ba, aa, conv_w, a_log, dt_bias, norm_w)
    yb = _swa_group(qb, kb, vb, zb, sinks)
    y = jnp.einsum('btc,cd->btd', jnp.concatenate([ya, yb], axis=-1), w_out)
    return _layernorm(DEEPNORM_ALPHA * x + y, ln_g, ln_b)


def setup_inputs(seed: int = 0) -> dict:
    key = jax.random.key(seed)
    ks = jax.random.split(key, 10)
    x = jax.random.normal(ks[0], (BATCH, SEQ, D_MODEL), jnp.float32)
    col_scale = np.concatenate([
        np.ones(2 * A_WIDTH), np.full(A_WIDTH, DEEPNORM_BETA), np.ones(A_WIDTH + 2 * A_HEADS),
        np.ones(B_WIDTH + B_KV_WIDTH), np.full(B_KV_WIDTH, DEEPNORM_BETA), np.ones(B_WIDTH)]).astype(np.float32)
    w_in = jax.random.normal(ks[1], (DEPTH, D_MODEL, IN_COLS), jnp.float32) * (D_MODEL ** -0.5) * jnp.asarray(col_scale)
    conv_w = jax.random.normal(ks[2], (DEPTH, CONV_K, 3 * A_WIDTH), jnp.float32) * (CONV_K ** -0.5)
    a_log = jnp.log(jax.random.uniform(ks[3], (DEPTH, A_HEADS), jnp.float32, 1.0, 16.0))
    dt = jnp.exp(jax.random.uniform(ks[4], (DEPTH, A_HEADS), jnp.float32, math.log(1e-3), math.log(1e-1)))
    dt_bias = dt + jnp.log(-jnp.expm1(-dt))
    norm_w = 1.0 + 0.02 * jax.random.normal(ks[5], (DEPTH, A_HEAD_DIM), jnp.float32)
    sinks = 0.5 * jax.random.normal(ks[6], (DEPTH, B_Q_HEADS), jnp.float32)
    w_out = jax.random.normal(ks[7], (DEPTH, D_MIX, D_MODEL), jnp.float32) * (D_MIX ** -0.5) * DEEPNORM_BETA
    ln_g = 1.0 + 0.02 * jax.random.normal(ks[8], (DEPTH, D_MODEL), jnp.float32)
    ln_b = 0.02 * jax.random.normal(ks[9], (DEPTH, D_MODEL), jnp.float32)
    return {"x": x, "w_in": w_in, "conv_w": conv_w, "a_log": a_log, "dt_bias": dt_bias,
            "norm_w": norm_w, "sinks": sinks, "w_out": w_out, "ln_g": ln_g, "ln_b": ln_b}


def reference(x, w_in, conv_w, a_log, dt_bias, norm_w, sinks, w_out, ln_g, ln_b):
    for l in range(DEPTH):
        x = _layer(x, w_in[l], conv_w[l], a_log[l], dt_bias[l], norm_w[l], sinks[l],
                   w_out[l], ln_g[l], ln_b[l])
    return x
```

```python
import functools

import jax
import jax.numpy as jnp
from jax import lax
from jax.experimental import pallas as pl
from jax.experimental.pallas import tpu as pltpu

F32 = jnp.float32
BF16 = jnp.bfloat16

LANES = 128
SUBLANES = 8
VMEM_LIMIT_BYTES = 48 * 1024 * 1024

A_HEADS = 4
A_HEAD_DIM = 128
A_WIDTH = A_HEADS * A_HEAD_DIM
CONV_K = 4
CHUNK = 64
B_Q_HEADS = 8
B_KV_HEADS = 2
B_HEAD_DIM = 64
B_GROUP = B_Q_HEADS // B_KV_HEADS
B_WIDTH = B_Q_HEADS * B_HEAD_DIM
B_KV_WIDTH = B_KV_HEADS * B_HEAD_DIM
WINDOW = 128
BLOCK = 128
LN_EPS = 1e-5
RMS_EPS = 1e-6
L2_EPS = 1e-6

COL_QA, COL_KA, COL_VA, COL_ZA = 0, A_WIDTH, 2 * A_WIDTH, 3 * A_WIDTH
COL_QB = 4 * A_WIDTH
COL_KB = COL_QB + B_WIDTH
COL_VB = COL_KB + B_KV_WIDTH
COL_ZB = COL_VB + B_KV_WIDTH
COL_SMALL = COL_ZB + B_WIDTH
N_COLS = COL_SMALL + LANES

TM_PROJ = 256
TB_DELTA = 256


def _silu(x):
    return x * (1.0 / (1.0 + jnp.exp(-x)))


def _sigmoid(x):
    return 1.0 / (1.0 + jnp.exp(-x))


def _softplus(x):
    return jnp.maximum(x, 0.0) + jnp.log(1.0 + jnp.exp(-jnp.abs(x)))


def _dot(a, b):
    return jnp.dot(a, b, preferred_element_type=F32)


def _dot_nt(a, b):
    return lax.dot_general(a, b, (((1,), (1,)), ((), ())), preferred_element_type=F32)


def _inproj_kernel(x_ref, w_ref, cw_ref, alog_ref, dtb_ref,
                   qa_ref, ka_ref, va_ref, ga_ref, bg_ref, qb_ref, kb_ref, vb_ref, gb_ref,
                   ext_ref, prev_ref):
    tm = x_ref.shape[1]
    t = pl.program_id(1)

    @pl.when(t == 0)
    def _():
        prev_ref[...] = jnp.zeros_like(prev_ref)

    xb = x_ref[0].astype(BF16)

    def proj(c0, n):
        return _dot(xb, w_ref[:, c0:c0 + n])

    for gi, out_ref in enumerate((qa_ref, ka_ref, va_ref)):
        c0 = gi * A_WIDTH
        h = proj(c0, A_WIDTH)
        ext_ref[0:SUBLANES, :] = prev_ref[gi]
        ext_ref[SUBLANES:SUBLANES + tm, :] = h
        prev_ref[gi] = h[tm - SUBLANES:tm, :]
        acc = h * cw_ref[CONV_K - 1:CONV_K, c0:c0 + A_WIDTH]
        for s in range(1, CONV_K):
            shifted = ext_ref[SUBLANES - s:SUBLANES - s + tm, :]
            acc = acc + shifted * cw_ref[CONV_K - 1 - s:CONV_K - s, c0:c0 + A_WIDTH]
        a = _silu(acc)
        if gi < 2:
            scale = A_HEAD_DIM ** -0.5 if gi == 0 else 1.0
            for hd in range(A_HEADS):
                sl = slice(hd * A_HEAD_DIM, (hd + 1) * A_HEAD_DIM)
                c = a[:, sl]
                ss = jnp.sum(c * c, axis=-1, keepdims=True)
                out_ref[0, :, sl] = (c * (lax.rsqrt(ss + L2_EPS) * scale)).astype(out_ref.dtype)
        else:
            out_ref[0] = a.astype(out_ref.dtype)

    ga_ref[0] = _silu(proj(COL_ZA, A_WIDTH)).astype(ga_ref.dtype)
    qb_ref[0] = (proj(COL_QB, B_WIDTH) * (B_HEAD_DIM ** -0.5)).astype(qb_ref.dtype)
    kb_ref[0] = proj(COL_KB, B_KV_WIDTH).astype(kb_ref.dtype)
    vb_ref[0] = proj(COL_VB, B_KV_WIDTH).astype(vb_ref.dtype)
    gb_ref[0] = _silu(proj(COL_ZB, B_WIDTH)).astype(gb_ref.dtype)

    hs = proj(COL_SMALL, LANES)
    lane = lax.broadcasted_iota(jnp.int32, hs.shape, 1)
    beta = _sigmoid(hs)
    g = -jnp.exp(alog_ref[...]) * _softplus(hs + dtb_ref[...])
    bg_ref[0] = jnp.where(lane < A_HEADS, beta, g)


def _inproj(x, w, cw, alog_vec, dtb_vec):
    bsz, t_len, d = x.shape
    tm = TM_PROJ
    grid = (bsz, t_len // tm)
    tok = lambda width: pl.BlockSpec((1, tm, width), lambda b, t: (b, t, 0))
    full = lambda a: pl.BlockSpec(a.shape, lambda b, t: (0,) * a.ndim)
    out_shapes = [
        jax.ShapeDtypeStruct((bsz, t_len, A_WIDTH), BF16),
        jax.ShapeDtypeStruct((bsz, t_len, A_WIDTH), BF16),
        jax.ShapeDtypeStruct((bsz, t_len, A_WIDTH), BF16),
        jax.ShapeDtypeStruct((bsz, t_len, A_WIDTH), BF16),
        jax.ShapeDtypeStruct((bsz, t_len, LANES), F32),
        jax.ShapeDtypeStruct((bsz, t_len, B_WIDTH), BF16),
        jax.ShapeDtypeStruct((bsz, t_len, B_KV_WIDTH), BF16),
        jax.ShapeDtypeStruct((bsz, t_len, B_KV_WIDTH), BF16),
        jax.ShapeDtypeStruct((bsz, t_len, B_WIDTH), BF16),
    ]
    out_specs = [tok(s.shape[-1]) for s in out_shapes]
    return pl.pallas_call(
        _inproj_kernel,
        grid=grid,
        in_specs=[tok(d), full(w), full(cw), full(alog_vec), full(dtb_vec)],
        out_specs=out_specs,
        out_shape=out_shapes,
        scratch_shapes=[
            pltpu.VMEM((tm + SUBLANES, A_WIDTH), F32),
            pltpu.VMEM((3, SUBLANES, A_WIDTH), F32),
        ],
        compiler_params=pltpu.CompilerParams(
            dimension_semantics=("arbitrary", "arbitrary"),
            vmem_limit_bytes=VMEM_LIMIT_BYTES),
        name="inproj",
    )(x, w, cw, alog_vec, dtb_vec)


def _split3(x):
    hi = x.astype(BF16)
    r1 = x - hi.astype(F32)
    mid = r1.astype(BF16)
    lo = (r1 - mid.astype(F32)).astype(BF16)
    return hi, mid, lo


def _delta_kernel(q_ref, k_ref, v_ref, bg_ref, gate_ref, nw_ref, o_ref, s_ref):
    tb = q_ref.shape[1]
    n_chunks = tb // CHUNK
    t = pl.program_id(1)

    @pl.when(t == 0)
    def _():
        s_ref[...] = jnp.zeros_like(s_ref)

    bg = bg_ref[0]
    r_i = lax.broadcasted_iota(jnp.int32, (tb, tb), 0)
    c_i = lax.broadcasted_iota(jnp.int32, (tb, tb), 1)
    same_chunk = (c_i & -CHUNK) == (r_i & -CHUNK)
    tril_blk = jnp.where((c_i <= r_i) & same_chunk, 1.0, 0.0).astype(BF16)
    hi, mid, lo = _split3(bg)
    gcum_all = _dot(tril_blk, hi) + _dot(tril_blk, mid) + _dot(tril_blk, lo)

    row = lax.broadcasted_iota(jnp.int32, (CHUNK, LANES), 0)
    lane = lax.broadcasted_iota(jnp.int32, (CHUNK, LANES), 1)
    col = jnp.where(lane >= CHUNK, lane - CHUNK, lane)
    causal = col <= row
    strict = col < row
    lo_half = lane < CHUNK
    eye_lo = jnp.where((lane == row), 1.0, 0.0)
    zeros_pad = jnp.zeros((CHUNK, LANES), F32)
    nw = nw_ref[...]

    for c in range(n_chunks):
        cs = slice(c * CHUNK, (c + 1) * CHUNK)
        bgc = bg[cs, :]
        gcum = gcum_all[cs, :]
        g2t = jnp.concatenate([gcum, gcum], axis=0).T
        for hd in range(A_HEADS):
            hs = slice(hd * A_HEAD_DIM, (hd + 1) * A_HEAD_DIM)
            q = q_ref[0, cs, hs]
            k = k_ref[0, cs, hs]
            v = v_ref[0, cs, hs]
            kf = k.astype(F32)
            gcol = gcum[:, A_HEADS + hd:A_HEADS + hd + 1]
            grow = g2t[A_HEADS + hd:A_HEADS + hd + 1, :]
            bcol = bgc[:, hd:hd + 1]
            glast = gcum[CHUNK - 1:CHUNK, A_HEADS + hd:A_HEADS + hd + 1]

            decay = jnp.where(causal, jnp.exp(jnp.minimum(gcol - grow, 0.0)), 0.0)
            k2 = jnp.concatenate([k, k], axis=0)
            qk_kk = _dot_nt(jnp.concatenate([q, k], axis=0), k2)
            qk = qk_kk[:CHUNK] * decay
            n_mat = jnp.where(strict, -(qk_kk[CHUNK:] * decay * bcol), 0.0)

            tp = jnp.where(lo_half, eye_lo, n_mat)
            for _ in range(6):
                lhs = jnp.where(lo_half, 0.0, tp).astype(BF16)
                tpb = tp.astype(BF16)
                r = _dot(lhs, jnp.concatenate([tpb, tpb], axis=0))
                tp = jnp.where(lo_half, tp, 0.0) + r
            t_lhs = jnp.where(lo_half, tp, 0.0).astype(BF16)

            eg = jnp.exp(gcol)
            vf = v.astype(F32)
            x_cat = jnp.concatenate([vf * bcol, kf * (bcol * eg)], axis=1).astype(BF16)
            zx = jnp.zeros_like(x_cat)
            uw = _dot(t_lhs, jnp.concatenate([x_cat, zx], axis=0))
            uwb = uw.astype(BF16)
            uw2 = jnp.concatenate([uwb, jnp.zeros_like(uwb)], axis=0)

            kd = kf * jnp.exp(glast - gcol)
            kdt = jnp.concatenate([kd, zeros_pad], axis=0).T.astype(BF16)
            cm_mm = _dot(kdt, uw2)
            qk_lhs = jnp.where(lo_half, qk, 0.0).astype(BF16)
            oi_qw = _dot(qk_lhs, uw2)
            qp = q.astype(F32) * eg - oi_qw[:, A_HEAD_DIM:]

            s_old = s_ref[hd]
            lhs_seq = jnp.concatenate([cm_mm[:, A_HEAD_DIM:], qp], axis=0).astype(BF16)
            r_seq = _dot(lhs_seq, s_old.astype(BF16))
            s_ref[hd] = jnp.exp(glast) * s_old + cm_mm[:, :A_HEAD_DIM] - r_seq[:A_HEAD_DIM]
            o = r_seq[A_HEAD_DIM:] + oi_qw[:, :A_HEAD_DIM]

            ms = jnp.mean(o * o, axis=-1, keepdims=True)
            y = o * lax.rsqrt(ms + RMS_EPS) * nw
            o_ref[0, cs, hs] = (y * gate_ref[0, cs, hs].astype(F32)).astype(o_ref.dtype)


def _delta(qa, ka, va, bg, gate, nw):
    bsz, t_len, _ = qa.shape
    tb = TB_DELTA
    grid = (bsz, t_len // tb)
    tok = lambda width: pl.BlockSpec((1, tb, width), lambda b, t: (b, t, 0))
    return pl.pallas_call(
        _delta_kernel,
        grid=grid,
        in_specs=[tok(A_WIDTH), tok(A_WIDTH), tok(A_WIDTH), tok(LANES), tok(A_WIDTH),
                  pl.BlockSpec(nw.shape, lambda b, t: (0, 0))],
        out_specs=tok(A_WIDTH),
        out_shape=jax.ShapeDtypeStruct((bsz, t_len, A_WIDTH), BF16),
        scratch_shapes=[pltpu.VMEM((A_HEADS, A_HEAD_DIM, A_HEAD_DIM), F32)],
        compiler_params=pltpu.CompilerParams(
            dimension_semantics=("arbitrary", "arbitrary"),
            vmem_limit_bytes=VMEM_LIMIT_BYTES),
        name="delta",
    )(qa, ka, va, bg, gate, nw)


def _swa_kernel(q_ref, kp_ref, kc_ref, vp_ref, vc_ref, gate_ref, slope_ref, sink_ref, o_ref):
    n = pl.program_id(1)
    lane_k = lax.broadcasted_iota(jnp.int32, (2 * BLOCK, LANES), 1)
    lo_k = lane_k < B_HEAD_DIM
    k_band = jnp.concatenate([kp_ref[0], kc_ref[0]], axis=0)
    v_band = jnp.concatenate([vp_ref[0], vc_ref[0]], axis=0)
    zero = jnp.zeros_like(k_band)
    k_rhs = jnp.concatenate([jnp.where(lo_k, k_band, zero), jnp.where(lo_k, zero, k_band)], axis=0)
    v_rhs = jnp.concatenate([jnp.where(lo_k, v_band, zero), jnp.where(lo_k, zero, v_band)], axis=0)

    q_idx = lax.broadcasted_iota(jnp.int32, (BLOCK, 4 * BLOCK), 0)
    s_all = lax.broadcasted_iota(jnp.int32, (BLOCK, 4 * BLOCK), 1)
    s_idx = jnp.where(s_all >= 2 * BLOCK, s_all - 2 * BLOCK, s_all)
    dist = q_idx + BLOCK - s_idx
    valid = (dist >= 0) & (dist < WINDOW) & ((s_idx >= BLOCK) | (n > 0))
    dist_f = dist.astype(F32)
    upper = s_all >= 2 * BLOCK
    lane_o = lax.broadcasted_iota(jnp.int32, (BLOCK, LANES), 1)

    for j in range(B_GROUP):
        ts = slice(j * LANES, (j + 1) * LANES)
        q = q_ref[0, :, ts]
        s = _dot_nt(q, k_rhs)
        slope = jnp.where(upper, slope_ref[1:2, ts][:, :1], slope_ref[0:1, ts][:, :1])
        s = jnp.where(valid, s - slope * dist_f, -jnp.inf)
        sink0 = sink_ref[0:1, ts][:, :1]
        sink1 = sink_ref[1:2, ts][:, :1]
        m0 = jnp.maximum(jnp.max(s[:, :2 * BLOCK], axis=-1, keepdims=True), sink0)
        m1 = jnp.maximum(jnp.max(s[:, 2 * BLOCK:], axis=-1, keepdims=True), sink1)
        m = jnp.where(upper, m1, m0)
        p = jnp.exp(s - m)
        d0 = jnp.sum(p[:, :2 * BLOCK], axis=-1, keepdims=True) + jnp.exp(sink0 - m0)
        d1 = jnp.sum(p[:, 2 * BLOCK:], axis=-1, keepdims=True) + jnp.exp(sink1 - m1)
        o = _dot(p.astype(BF16), v_rhs)
        inv = jnp.where(lane_o < B_HEAD_DIM, 1.0 / d0, 1.0 / d1)
        o_ref[0, :, ts] = (o * inv * gate_ref[0, :, ts].astype(F32)).astype(o_ref.dtype)


def _swa(qb, kb, vb, gate, slope_rows, sink_rows):
    bsz, t_len, _ = qb.shape
    grid = (bsz, t_len // BLOCK)
    cur = lambda width: pl.BlockSpec((1, BLOCK, width), lambda b, n: (b, n, 0))
    prev = lambda width: pl.BlockSpec((1, BLOCK, width), lambda b, n: (b, jnp.maximum(n - 1, 0), 0))
    full = lambda a: pl.BlockSpec(a.shape, lambda b, n: (0,) * a.ndim)
    return pl.pallas_call(
        _swa_kernel,
        grid=grid,
        in_specs=[cur(B_WIDTH), prev(B_KV_WIDTH), cur(B_KV_WIDTH), prev(B_KV_WIDTH), cur(B_KV_WIDTH),
                  cur(B_WIDTH), full(slope_rows), full(sink_rows)],
        out_specs=cur(B_WIDTH),
        out_shape=jax.ShapeDtypeStruct((bsz, t_len, B_WIDTH), BF16),
        compiler_params=pltpu.CompilerParams(
            dimension_semantics=("arbitrary", "arbitrary"),
            vmem_limit_bytes=VMEM_LIMIT_BYTES),
        name="swa",
    )(qb, kb, kb, vb, vb, gate, slope_rows, sink_rows)


def _outproj_kernel(alpha, x_ref, ya_ref, yb_ref, wa_ref, wb_ref, g_ref, b_ref, o_ref):
    y = _dot(ya_ref[0], wa_ref[...]) + _dot(yb_ref[0], wb_ref[...])
    r = alpha * x_ref[0] + y
    mu = jnp.mean(r, axis=-1, keepdims=True)
    rc = r - mu
    var = jnp.mean(rc * rc, axis=-1, keepdims=True)
    o_ref[0] = rc * lax.rsqrt(var + LN_EPS) * g_ref[...] + b_ref[...]


def _outproj(x, ya, yb, wa, wb, ln_g, ln_b, alpha):
    bsz, t_len, d = x.shape
    tm = TM_PROJ
    grid = (bsz, t_len // tm)
    tok = lambda width: pl.BlockSpec((1, tm, width), lambda b, t: (b, t, 0))
    full = lambda a: pl.BlockSpec(a.shape, lambda b, t: (0,) * a.ndim)
    return pl.pallas_call(
        functools.partial(_outproj_kernel, alpha),
        grid=grid,
        in_specs=[tok(d), tok(A_WIDTH), tok(B_WIDTH), full(wa), full(wb), full(ln_g), full(ln_b)],
        out_specs=tok(d),
        out_shape=jax.ShapeDtypeStruct((bsz, t_len, d), F32),
        compiler_params=pltpu.CompilerParams(
            dimension_semantics=("arbitrary", "arbitrary"),
            vmem_limit_bytes=VMEM_LIMIT_BYTES),
        name="outproj",
    )(x, ya, yb, wa, wb, ln_g, ln_b)


def _tile_head_order():
    order = []
    for j in range(B_GROUP):
        order += [j, B_GROUP + j]
    return order


def _permute_heads(a, axis):
    parts = jnp.split(a, B_Q_HEADS, axis=axis)
    return jnp.concatenate([parts[h] for h in _tile_head_order()], axis=axis)


def _prep_layer(w_in, conv_w, a_log, dt_bias, norm_w, sinks, w_out, ln_g, ln_b):
    d = w_in.shape[0]
    o_ba = 4 * A_WIDTH
    o_qb = o_ba + 2 * A_HEADS
    o_kb = o_qb + B_WIDTH
    o_vb = o_kb + B_KV_WIDTH
    o_zb = o_vb + B_KV_WIDTH
    small = jnp.concatenate(
        [w_in[:, o_ba:o_qb], jnp.zeros((d, LANES - 2 * A_HEADS), w_in.dtype)], axis=1)
    w = jnp.concatenate([
        w_in[:, :o_ba],
        _permute_heads(w_in[:, o_qb:o_kb], 1),
        w_in[:, o_kb:o_zb],
        _permute_heads(w_in[:, o_zb:o_zb + B_WIDTH], 1),
        small], axis=1).astype(BF16)
    pad = jnp.zeros((LANES - 2 * A_HEADS,), F32)
    zeros_h = jnp.zeros((A_HEADS,), F32)
    alog_vec = jnp.concatenate([zeros_h, a_log.astype(F32), pad])[None, :]
    dtb_vec = jnp.concatenate([zeros_h, dt_bias.astype(F32), pad])[None, :]
    slopes = jnp.asarray([2.0 ** (-8.0 * (h + 1) / B_Q_HEADS) for h in range(B_Q_HEADS)], F32)

    def rows(vals):
        per = vals.astype(F32).reshape(B_KV_HEADS, B_GROUP)
        return jnp.repeat(per, LANES, axis=1)

    wa = w_out[:A_WIDTH].astype(BF16)
    wb = _permute_heads(w_out[A_WIDTH:], 0).astype(BF16)
    return dict(w=w, cw=conv_w.astype(F32), alog_vec=alog_vec, dtb_vec=dtb_vec,
                nw=norm_w.astype(F32)[None, :], slope_rows=rows(slopes), sink_rows=rows(sinks),
                wa=wa, wb=wb, ln_g=ln_g.astype(F32)[None, :], ln_b=ln_b.astype(F32)[None, :])


def kernel(x, w_in, conv_w, a_log, dt_bias, norm_w, sinks, w_out, ln_g, ln_b):
    depth = w_in.shape[0]
    alpha = (2 * depth) ** 0.25
    for l in range(depth):
        p = _prep_layer(w_in[l], conv_w[l], a_log[l], dt_bias[l], norm_w[l], sinks[l],
                        w_out[l], ln_g[l], ln_b[l])
        qa, ka, va, ga, bg, qb, kb, vb, gb = _inproj(x, p["w"], p["cw"], p["alog_vec"], p["dtb_vec"])
        ya = _delta(qa, ka, va, bg, ga, p["nw"])
        yb = _swa(qb, kb, vb, gb, p["slope_rows"], p["sink_rows"])
        x = _outproj(x, ya, yb, p["wa"], p["wb"], p["ln_g"], p["ln_b"], alpha)
    return x
```

```python
import functools

import jax
import jax.numpy as jnp
from jax import lax
from jax.experimental import pallas as pl
from jax.experimental.pallas import tpu as pltpu

F32 = jnp.float32
BF16 = jnp.bfloat16

LANES = 128
SUBLANES = 8
VMEM_LIMIT_BYTES = 48 * 1024 * 1024

A_HEADS = 4
A_HEAD_DIM = 128
A_WIDTH = A_HEADS * A_HEAD_DIM
CONV_K = 4
CHUNK = 64
B_Q_HEADS = 8
B_KV_HEADS = 2
B_HEAD_DIM = 64
B_GROUP = B_Q_HEADS // B_KV_HEADS
B_WIDTH = B_Q_HEADS * B_HEAD_DIM
B_KV_WIDTH = B_KV_HEADS * B_HEAD_DIM
WINDOW = 128
BLOCK = 128
LN_EPS = 1e-5
RMS_EPS = 1e-6
L2_EPS = 1e-6

COL_QA, COL_KA, COL_VA, COL_ZA = 0, A_WIDTH, 2 * A_WIDTH, 3 * A_WIDTH
COL_QB = 4 * A_WIDTH
COL_KB = COL_QB + B_WIDTH
COL_VB = COL_KB + B_KV_WIDTH
COL_ZB = COL_VB + B_KV_WIDTH
COL_SMALL = COL_ZB + B_WIDTH
N_COLS = COL_SMALL + LANES

TM_PROJ = 256
TB_DELTA = 256


def _silu(x):
    return x * (1.0 / (1.0 + jnp.exp(-x)))


def _sigmoid(x):
    return 1.0 / (1.0 + jnp.exp(-x))


def _softplus(x):
    return jnp.maximum(x, 0.0) + jnp.log(1.0 + jnp.exp(-jnp.abs(x)))


def _dot(a, b):
    return jnp.dot(a, b, preferred_element_type=F32)


def _dot_nt(a, b):
    return lax.dot_general(a, b, (((1,), (1,)), ((), ())), preferred_element_type=F32)


def _inproj_kernel(x_ref, w_ref, cw_ref, alog_ref, dtb_ref,
                   qa_ref, ka_ref, va_ref, ga_ref, bg_ref, qb_ref, kb_ref, vb_ref, gb_ref,
                   ext_ref, prev_ref):
    tm = x_ref.shape[1]
    t = pl.program_id(1)

    @pl.when(t == 0)
    def _():
        prev_ref[...] = jnp.zeros_like(prev_ref)

    xb = x_ref[0].astype(BF16)

    def proj(c0, n):
        return _dot(xb, w_ref[:, c0:c0 + n])

    for gi, out_ref in enumerate((qa_ref, ka_ref, va_ref)):
        c0 = gi * A_WIDTH
        h = proj(c0, A_WIDTH)
        ext_ref[0:SUBLANES, :] = prev_ref[gi]
        ext_ref[SUBLANES:SUBLANES + tm, :] = h
        prev_ref[gi] = h[tm - SUBLANES:tm, :]
        acc = h * cw_ref[CONV_K - 1:CONV_K, c0:c0 + A_WIDTH]
        for s in range(1, CONV_K):
            shifted = ext_ref[SUBLANES - s:SUBLANES - s + tm, :]
            acc = acc + shifted * cw_ref[CONV_K - 1 - s:CONV_K - s, c0:c0 + A_WIDTH]
        a = _silu(acc)
        if gi < 2:
            scale = A_HEAD_DIM ** -0.5 if gi == 0 else 1.0
            for hd in range(A_HEADS):
                sl = slice(hd * A_HEAD_DIM, (hd + 1) * A_HEAD_DIM)
                c = a[:, sl]
                ss = jnp.sum(c * c, axis=-1, keepdims=True)
                out_ref[0, :, sl] = (c * (lax.rsqrt(ss + L2_EPS) * scale)).astype(out_ref.dtype)
        else:
            out_ref[0] = a.astype(out_ref.dtype)

    ga_ref[0] = _silu(proj(COL_ZA, A_WIDTH)).astype(ga_ref.dtype)
    qb_ref[0] = (proj(COL_QB, B_WIDTH) * (B_HEAD_DIM ** -0.5)).astype(qb_ref.dtype)
    kb_ref[0] = proj(COL_KB, B_KV_WIDTH).astype(kb_ref.dtype)
    vb_ref[0] = proj(COL_VB, B_KV_WIDTH).astype(vb_ref.dtype)
    gb_ref[0] = _silu(proj(COL_ZB, B_WIDTH)).astype(gb_ref.dtype)

    hs = proj(COL_SMALL, LANES)
    lane = lax.broadcasted_iota(jnp.int32, hs.shape, 1)
    beta = _sigmoid(hs)
    g = -jnp.exp(alog_ref[...]) * _softplus(hs + dtb_ref[...])
    bg_ref[0] = jnp.where(lane < A_HEADS, beta, g)


def _inproj(x, w, cw, alog_vec, dtb_vec):
    bsz, t_len, d = x.shape
    tm = TM_PROJ
    grid = (bsz, t_len // tm)
    tok = lambda width: pl.BlockSpec((1, tm, width), lambda b, t: (b, t, 0))
    full = lambda a: pl.BlockSpec(a.shape, lambda b, t: (0,) * a.ndim)
    out_shapes = [
        jax.ShapeDtypeStruct((bsz, t_len, A_WIDTH), BF16),
        jax.ShapeDtypeStruct((bsz, t_len, A_WIDTH), BF16),
        jax.ShapeDtypeStruct((bsz, t_len, A_WIDTH), BF16),
        jax.ShapeDtypeStruct((bsz, t_len, A_WIDTH), BF16),
        jax.ShapeDtypeStruct((bsz, t_len, LANES), F32),
        jax.ShapeDtypeStruct((bsz, t_len, B_WIDTH), BF16),
        jax.ShapeDtypeStruct((bsz, t_len, B_KV_WIDTH), BF16),
        jax.ShapeDtypeStruct((bsz, t_len, B_KV_WIDTH), BF16),
        jax.ShapeDtypeStruct((bsz, t_len, B_WIDTH), BF16),
    ]
    out_specs = [tok(s.shape[-1]) for s in out_shapes]
    return pl.pallas_call(
        _inproj_kernel,
        grid=grid,
        in_specs=[tok(d), full(w), full(cw), full(alog_vec), full(dtb_vec)],
        out_specs=out_specs,
        out_shape=out_shapes,
        scratch_shapes=[
            pltpu.VMEM((tm + SUBLANES, A_WIDTH), F32),
            pltpu.VMEM((3, SUBLANES, A_WIDTH), F32),
        ],
        compiler_params=pltpu.CompilerParams(
            dimension_semantics=("arbitrary", "arbitrary"),
            vmem_limit_bytes=VMEM_LIMIT_BYTES),
        name="inproj",
    )(x, w, cw, alog_vec, dtb_vec)


def _split3(x):
    hi = x.astype(BF16)
    r1 = x - hi.astype(F32)
    mid = r1.astype(BF16)
    lo = (r1 - mid.astype(F32)).astype(BF16)
    return hi, mid, lo


def _delta_kernel(q_ref, k_ref, v_ref, bg_ref, gate_ref, nw_ref, o_ref, s_ref):
    tb = q_ref.shape[1]
    n_chunks = tb // CHUNK
    t = pl.program_id(1)

    @pl.when(t == 0)
    def _():
        s_ref[...] = jnp.zeros_like(s_ref)

    bg = bg_ref[0]
    r_i = lax.broadcasted_iota(jnp.int32, (tb, tb), 0)
    c_i = lax.broadcasted_iota(jnp.int32, (tb, tb), 1)
    same_chunk = (c_i & -CHUNK) == (r_i & -CHUNK)
    tril_blk = jnp.where((c_i <= r_i) & same_chunk, 1.0, 0.0).astype(BF16)
    hi, mid, lo = _split3(bg)
    gcum_all = _dot(tril_blk, hi) + _dot(tril_blk, mid) + _dot(tril_blk, lo)

    row = lax.broadcasted_iota(jnp.int32, (CHUNK, LANES), 0)
    lane = lax.broadcasted_iota(jnp.int32, (CHUNK, LANES), 1)
    col = jnp.where(lane >= CHUNK, lane - CHUNK, lane)
    causal = col <= row
    strict = col < row
    lo_half = lane < CHUNK
    eye_lo = jnp.where((lane == row), 1.0, 0.0)
    zeros_pad = jnp.zeros((CHUNK, LANES), F32)
    nw = nw_ref[...]

    pairs = [(c, hd) for c in range(n_chunks) for hd in range(A_HEADS)]
    csl = lambda c: slice(c * CHUNK, (c + 1) * CHUNK)
    hsl = lambda hd: slice(hd * A_HEAD_DIM, (hd + 1) * A_HEAD_DIM)
    gcums = [gcum_all[csl(c), :] for c in range(n_chunks)]
    g2ts = [jnp.concatenate([g, g], axis=0).T for g in gcums]

    st = {}
    for (c, hd) in pairs:
        gcum = gcums[c]
        q = q_ref[0, csl(c), hsl(hd)]
        k = k_ref[0, csl(c), hsl(hd)]
        gcol = gcum[:, A_HEADS + hd:A_HEADS + hd + 1]
        grow = g2ts[c][A_HEADS + hd:A_HEADS + hd + 1, :]
        bcol = bg[csl(c), hd:hd + 1]
        glast = gcum[CHUNK - 1:CHUNK, A_HEADS + hd:A_HEADS + hd + 1]
        decay = jnp.where(causal, jnp.exp(jnp.minimum(gcol - grow, 0.0)), 0.0)
        k2 = jnp.concatenate([k, k], axis=0)
        qk_kk = _dot_nt(jnp.concatenate([q, k], axis=0), k2)
        qk = qk_kk[:CHUNK] * decay
        n_mat = jnp.where(strict, -(qk_kk[CHUNK:] * decay * bcol), 0.0)
        st[(c, hd)] = dict(q=q, k=k, gcol=gcol, bcol=bcol, glast=glast, qk=qk,
                           tp=jnp.where(lo_half, eye_lo, n_mat))

    for _ in range(6):
        for p in pairs:
            tp = st[p]["tp"]
            lhs = jnp.where(lo_half, 0.0, tp).astype(BF16)
            tpb = tp.astype(BF16)
            r = _dot(lhs, jnp.concatenate([tpb, tpb], axis=0))
            st[p]["tp"] = jnp.where(lo_half, tp, 0.0) + r

    for p in pairs:
        d = st[p]
        t_lhs = jnp.where(lo_half, d["tp"], 0.0).astype(BF16)
        kf = d["k"].astype(F32)
        vf = v_ref[0, csl(p[0]), hsl(p[1])].astype(F32)
        eg = jnp.exp(d["gcol"])
        x_cat = jnp.concatenate([vf * d["bcol"], kf * (d["bcol"] * eg)], axis=1).astype(BF16)
        uw = _dot(t_lhs, jnp.concatenate([x_cat, jnp.zeros_like(x_cat)], axis=0))
        uwb = uw.astype(BF16)
        d["uw2"] = jnp.concatenate([uwb, jnp.zeros_like(uwb)], axis=0)
        d["eg"] = eg
        kd = kf * jnp.exp(d["glast"] - d["gcol"])
        d["kdt"] = jnp.concatenate([kd, zeros_pad], axis=0).T.astype(BF16)

    for p in pairs:
        d = st[p]
        d["cm_mm"] = _dot(d["kdt"], d["uw2"])
        qk_lhs = jnp.where(lo_half, d["qk"], 0.0).astype(BF16)
        oi_qw = _dot(qk_lhs, d["uw2"])
        d["oi"] = oi_qw[:, :A_HEAD_DIM]
        d["qp"] = d["q"].astype(F32) * d["eg"] - oi_qw[:, A_HEAD_DIM:]

    for (c, hd) in pairs:
        d = st[(c, hd)]
        s_old = s_ref[hd]
        lhs_seq = jnp.concatenate([d["cm_mm"][:, A_HEAD_DIM:], d["qp"]], axis=0).astype(BF16)
        r_seq = _dot(lhs_seq, s_old.astype(BF16))
        s_ref[hd] = jnp.exp(d["glast"]) * s_old + d["cm_mm"][:, :A_HEAD_DIM] - r_seq[:A_HEAD_DIM]
        o = r_seq[A_HEAD_DIM:] + d["oi"]
        ms = jnp.mean(o * o, axis=-1, keepdims=True)
        y = o * lax.rsqrt(ms + RMS_EPS) * nw
        gate = gate_ref[0, csl(c), hsl(hd)].astype(F32)
        o_ref[0, csl(c), hsl(hd)] = (y * gate).astype(o_ref.dtype)


def _delta(qa, ka, va, bg, gate, nw):
    bsz, t_len, _ = qa.shape
    tb = TB_DELTA
    grid = (bsz, t_len // tb)
    tok = lambda width: pl.BlockSpec((1, tb, width), lambda b, t: (b, t, 0))
    return pl.pallas_call(
        _delta_kernel,
        grid=grid,
        in_specs=[tok(A_WIDTH), tok(A_WIDTH), tok(A_WIDTH), tok(LANES), tok(A_WIDTH),
                  pl.BlockSpec(nw.shape, lambda b, t: (0, 0))],
        out_specs=tok(A_WIDTH),
        out_shape=jax.ShapeDtypeStruct((bsz, t_len, A_WIDTH), BF16),
        scratch_shapes=[pltpu.VMEM((A_HEADS, A_HEAD_DIM, A_HEAD_DIM), F32)],
        compiler_params=pltpu.CompilerParams(
            dimension_semantics=("arbitrary", "arbitrary"),
            vmem_limit_bytes=VMEM_LIMIT_BYTES),
        name="delta",
    )(qa, ka, va, bg, gate, nw)


def _swa_kernel(q_ref, kp_ref, kc_ref, vp_ref, vc_ref, gate_ref, slope_ref, sink_ref, o_ref):
    n = pl.program_id(1)
    lane_k = lax.broadcasted_iota(jnp.int32, (2 * BLOCK, LANES), 1)
    lo_k = lane_k < B_HEAD_DIM
    k_band = jnp.concatenate([kp_ref[0], kc_ref[0]], axis=0)
    v_band = jnp.concatenate([vp_ref[0], vc_ref[0]], axis=0)
    zero = jnp.zeros_like(k_band)
    k_rhs = jnp.concatenate([jnp.where(lo_k, k_band, zero), jnp.where(lo_k, zero, k_band)], axis=0)
    v_rhs = jnp.concatenate([jnp.where(lo_k, v_band, zero), jnp.where(lo_k, zero, v_band)], axis=0)

    q_idx = lax.broadcasted_iota(jnp.int32, (BLOCK, 4 * BLOCK), 0)
    s_all = lax.broadcasted_iota(jnp.int32, (BLOCK, 4 * BLOCK), 1)
    s_idx = jnp.where(s_all >= 2 * BLOCK, s_all - 2 * BLOCK, s_all)
    dist = q_idx + BLOCK - s_idx
    valid = (dist >= 0) & (dist < WINDOW) & ((s_idx >= BLOCK) | (n > 0))
    dist_f = dist.astype(F32)
    upper = s_all >= 2 * BLOCK
    lane_o = lax.broadcasted_iota(jnp.int32, (BLOCK, LANES), 1)

    for j in range(B_GROUP):
        ts = slice(j * LANES, (j + 1) * LANES)
        q = q_ref[0, :, ts]
        s = _dot_nt(q, k_rhs)
        slope = jnp.where(upper, slope_ref[1:2, ts][:, :1], slope_ref[0:1, ts][:, :1])
        s = jnp.where(valid, s - slope * dist_f, -jnp.inf)
        sink0 = sink_ref[0:1, ts][:, :1]
        sink1 = sink_ref[1:2, ts][:, :1]
        m0 = jnp.maximum(jnp.max(s[:, :2 * BLOCK], axis=-1, keepdims=True), sink0)
        m1 = jnp.maximum(jnp.max(s[:, 2 * BLOCK:], axis=-1, keepdims=True), sink1)
        m = jnp.where(upper, m1, m0)
        p = jnp.exp(s - m)
        d0 = jnp.sum(p[:, :2 * BLOCK], axis=-1, keepdims=True) + jnp.exp(sink0 - m0)
        d1 = jnp.sum(p[:, 2 * BLOCK:], axis=-1, keepdims=True) + jnp.exp(sink1 - m1)
        o = _dot(p.astype(BF16), v_rhs)
        inv = jnp.where(lane_o < B_HEAD_DIM, 1.0 / d0, 1.0 / d1)
        o_ref[0, :, ts] = (o * inv * gate_ref[0, :, ts].astype(F32)).astype(o_ref.dtype)


def _swa(qb, kb, vb, gate, slope_rows, sink_rows):
    bsz, t_len, _ = qb.shape
    grid = (bsz, t_len // BLOCK)
    cur = lambda width: pl.BlockSpec((1, BLOCK, width), lambda b, n: (b, n, 0))
    prev = lambda width: pl.BlockSpec((1, BLOCK, width), lambda b, n: (b, jnp.maximum(n - 1, 0), 0))
    full = lambda a: pl.BlockSpec(a.shape, lambda b, n: (0,) * a.ndim)
    return pl.pallas_call(
        _swa_kernel,
        grid=grid,
        in_specs=[cur(B_WIDTH), prev(B_KV_WIDTH), cur(B_KV_WIDTH), prev(B_KV_WIDTH), cur(B_KV_WIDTH),
                  cur(B_WIDTH), full(slope_rows), full(sink_rows)],
        out_specs=cur(B_WIDTH),
        out_shape=jax.ShapeDtypeStruct((bsz, t_len, B_WIDTH), BF16),
        compiler_params=pltpu.CompilerParams(
            dimension_semantics=("arbitrary", "arbitrary"),
            vmem_limit_bytes=VMEM_LIMIT_BYTES),
        name="swa",
    )(qb, kb, kb, vb, vb, gate, slope_rows, sink_rows)


def _outproj_kernel(alpha, x_ref, ya_ref, yb_ref, wa_ref, wb_ref, g_ref, b_ref, o_ref):
    y = _dot(ya_ref[0], wa_ref[...]) + _dot(yb_ref[0], wb_ref[...])
    r = alpha * x_ref[0] + y
    mu = jnp.mean(r, axis=-1, keepdims=True)
    rc = r - mu
    var = jnp.mean(rc * rc, axis=-1, keepdims=True)
    o_ref[0] = rc * lax.rsqrt(var + LN_EPS) * g_ref[...] + b_ref[...]


def _outproj(x, ya, yb, wa, wb, ln_g, ln_b, alpha):
    bsz, t_len, d = x.shape
    tm = TM_PROJ
    grid = (bsz, t_len // tm)
    tok = lambda width: pl.BlockSpec((1, tm, width), lambda b, t: (b, t, 0))
    full = lambda a: pl.BlockSpec(a.shape, lambda b, t: (0,) * a.ndim)
    return pl.pallas_call(
        functools.partial(_outproj_kernel, alpha),
        grid=grid,
        in_specs=[tok(d), tok(A_WIDTH), tok(B_WIDTH), full(wa), full(wb), full(ln_g), full(ln_b)],
        out_specs=tok(d),
        out_shape=jax.ShapeDtypeStruct((bsz, t_len, d), F32),
        compiler_params=pltpu.CompilerParams(
            dimension_semantics=("arbitrary", "arbitrary"),
            vmem_limit_bytes=VMEM_LIMIT_BYTES),
        name="outproj",
    )(x, ya, yb, wa, wb, ln_g, ln_b)


def _tile_head_order():
    order = []
    for j in range(B_GROUP):
        order += [j, B_GROUP + j]
    return order


def _permute_heads(a, axis):
    parts = jnp.split(a, B_Q_HEADS, axis=axis)
    return jnp.concatenate([parts[h] for h in _tile_head_order()], axis=axis)


def _prep_layer(w_in, conv_w, a_log, dt_bias, norm_w, sinks, w_out, ln_g, ln_b):
    d = w_in.shape[0]
    o_ba = 4 * A_WIDTH
    o_qb = o_ba + 2 * A_HEADS
    o_kb = o_qb + B_WIDTH
    o_vb = o_kb + B_KV_WIDTH
    o_zb = o_vb + B_KV_WIDTH
    small = jnp.concatenate(
        [w_in[:, o_ba:o_qb], jnp.zeros((d, LANES - 2 * A_HEADS), w_in.dtype)], axis=1)
    w = jnp.concatenate([
        w_in[:, :o_ba],
        _permute_heads(w_in[:, o_qb:o_kb], 1),
        w_in[:, o_kb:o_zb],
        _permute_heads(w_in[:, o_zb:o_zb + B_WIDTH], 1),
        small], axis=1).astype(BF16)
    pad = jnp.zeros((LANES - 2 * A_HEADS,), F32)
    zeros_h = jnp.zeros((A_HEADS,), F32)
    alog_vec = jnp.concatenate([zeros_h, a_log.astype(F32), pad])[None, :]
    dtb_vec = jnp.concatenate([zeros_h, dt_bias.astype(F32), pad])[None, :]
    slopes = jnp.asarray([2.0 ** (-8.0 * (h + 1) / B_Q_HEADS) for h in range(B_Q_HEADS)], F32)

    def rows(vals):
        per = vals.astype(F32).reshape(B_KV_HEADS, B_GROUP)
        return jnp.repeat(per, LANES, axis=1)

    wa = w_out[:A_WIDTH].astype(BF16)
    wb = _permute_heads(w_out[A_WIDTH:], 0).astype(BF16)
    return dict(w=w, cw=conv_w.astype(F32), alog_vec=alog_vec, dtb_vec=dtb_vec,
                nw=norm_w.astype(F32)[None, :], slope_rows=rows(slopes), sink_rows=rows(sinks),
                wa=wa, wb=wb, ln_g=ln_g.astype(F32)[None, :], ln_b=ln_b.astype(F32)[None, :])


def kernel(x, w_in, conv_w, a_log, dt_bias, norm_w, sinks, w_out, ln_g, ln_b):
    depth = w_in.shape[0]
    alpha = (2 * depth) ** 0.25
    for l in range(depth):
        p = _prep_layer(w_in[l], conv_w[l], a_log[l], dt_bias[l], norm_w[l], sinks[l],
                        w_out[l], ln_g[l], ln_b[l])
        qa, ka, va, ga, bg, qb, kb, vb, gb = _inproj(x, p["w"], p["cw"], p["alog_vec"], p["dtb_vec"])
        ya = _delta(qa, ka, va, bg, ga, p["nw"])
        yb = _swa(qb, kb, vb, gb, p["slope_rows"], p["sink_rows"])
        x = _outproj(x, ya, yb, p["wa"], p["wb"], p["ln_g"], p["ln_b"], alpha)
    return x
```

```python
import functools

import jax
import jax.numpy as jnp
from jax import lax
from jax.experimental import pallas as pl
from jax.experimental.pallas import tpu as pltpu

F32 = jnp.float32
BF16 = jnp.bfloat16

LANES = 128
SUBLANES = 8
VMEM_LIMIT_BYTES = 48 * 1024 * 1024

A_HEADS = 4
A_HEAD_DIM = 128
A_WIDTH = A_HEADS * A_HEAD_DIM
CONV_K = 4
CHUNK = 64
B_Q_HEADS = 8
B_KV_HEADS = 2
B_HEAD_DIM = 64
B_GROUP = B_Q_HEADS // B_KV_HEADS
B_WIDTH = B_Q_HEADS * B_HEAD_DIM
B_KV_WIDTH = B_KV_HEADS * B_HEAD_DIM
WINDOW = 128
BLOCK = 128
LN_EPS = 1e-5
RMS_EPS = 1e-6
L2_EPS = 1e-6

COLS_A = 4 * A_WIDTH
COL_KB = 0
COL_ZB = COL_KB + B_KV_WIDTH
COL_SMALL = COL_ZB + B_WIDTH
COLS_B = COL_SMALL + LANES

TM_PROJ = 512
CONV_ROWS = 32
TM_OUT = 512
TB_DELTA = 512
QB_SWA = 4


def _silu(x):
    return x * (1.0 / (1.0 + jnp.exp(-x)))


def _sigmoid(x):
    return 1.0 / (1.0 + jnp.exp(-x))


def _softplus(x):
    return jnp.maximum(x, 0.0) + jnp.log(1.0 + jnp.exp(-jnp.abs(x)))


def _dot(a, b):
    return jnp.dot(a, b, preferred_element_type=F32)


def _dot_nt(a, b):
    return lax.dot_general(a, b, (((1,), (1,)), ((), ())), preferred_element_type=F32)


def _inproj_kernel(x_ref, wa_ref, wb_ref, wt_ref, cw_ref, alog_ref, dtb_ref,
                   qa_ref, ka_ref, va_ref, za_ref, bg_ref, qbt_ref, kb_ref, vbt_ref, zb_ref,
                   ext_ref, prev_ref):
    tm = x_ref.shape[1]
    t = pl.program_id(1)

    @pl.when(t == 0)
    def _():
        prev_ref[...] = jnp.zeros_like(prev_ref)

    xb = x_ref[0].astype(BF16)

    def proj_a(c0, n):
        return _dot(xb, wa_ref[:, c0:c0 + n])

    def proj_b(c0, n):
        return _dot(xb, wb_ref[:, c0:c0 + n])

    def conv_group(gi, h, out_ref):
        c0 = gi * A_WIDTH
        ext_ref[gi, 0:SUBLANES, :] = prev_ref[gi]
        ext_ref[gi, SUBLANES:SUBLANES + tm, :] = h
        prev_ref[gi] = ext_ref[gi, tm:tm + SUBLANES, :]
        scale = A_HEAD_DIM ** -0.5 if gi == 0 else 1.0
        for hd in range(A_HEADS):
            sl = slice(hd * A_HEAD_DIM, (hd + 1) * A_HEAD_DIM)
            wsl = slice(c0 + hd * A_HEAD_DIM, c0 + (hd + 1) * A_HEAD_DIM)
            taps = [cw_ref[CONV_K - 1 - s:CONV_K - s, wsl] for s in range(CONV_K)]
            for r0 in range(0, tm, CONV_ROWS):
                acc = None
                for s in range(CONV_K):
                    term = ext_ref[gi, SUBLANES + r0 - s:SUBLANES + r0 - s + CONV_ROWS, sl] * taps[s]
                    acc = term if acc is None else acc + term
                a = _silu(acc)
                if gi < 2:
                    ss = jnp.sum(a * a, axis=-1, keepdims=True)
                    a = a * (lax.rsqrt(ss + L2_EPS) * scale)
                out_ref[0, r0:r0 + CONV_ROWS, sl] = a.astype(out_ref.dtype)

    h_q = proj_a(0 * A_WIDTH, A_WIDTH)
    h_za = proj_a(3 * A_WIDTH, A_WIDTH)
    conv_group(0, h_q, qa_ref)
    h_k = proj_a(1 * A_WIDTH, A_WIDTH)
    h_t = _dot_nt(wt_ref[...], xb)
    za_ref[0] = h_za.astype(za_ref.dtype)
    conv_group(1, h_k, ka_ref)
    h_v = proj_a(2 * A_WIDTH, A_WIDTH)
    h_zb = proj_b(COL_ZB, B_WIDTH)
    qbt_ref[0] = (h_t[:B_WIDTH] * (B_HEAD_DIM ** -0.5)).astype(qbt_ref.dtype)
    vbt_ref[0] = h_t[B_WIDTH:].astype(vbt_ref.dtype)
    conv_group(2, h_v, va_ref)
    h_kb = proj_b(COL_KB, B_KV_WIDTH)
    hs = proj_b(COL_SMALL, LANES)
    zb_ref[0] = h_zb.astype(zb_ref.dtype)
    kb_ref[0] = h_kb.astype(kb_ref.dtype)

    lane = lax.broadcasted_iota(jnp.int32, hs.shape, 1)
    beta = _sigmoid(hs)
    g = -jnp.exp(alog_ref[...]) * _softplus(hs + dtb_ref[...])
    bg_ref[0] = jnp.where(lane < A_HEADS, beta, g)


def _inproj(x, w_in_all, layer, wb, wt, cw, alog_vec, dtb_vec):
    bsz, t_len, d = x.shape
    tm = TM_PROJ
    grid = (bsz, t_len // tm)
    tok = lambda width: pl.BlockSpec((1, tm, width), lambda b, t: (b, t, 0))
    tok_t = lambda rows: pl.BlockSpec((1, rows, tm), lambda b, t: (b, 0, t))
    full = lambda a: pl.BlockSpec(a.shape, lambda b, t: (0,) * a.ndim)
    wa_spec = pl.BlockSpec((None, d, COLS_A), lambda b, t: (layer, 0, 0))
    out_shapes = [
        jax.ShapeDtypeStruct((bsz, t_len, A_WIDTH), BF16),
        jax.ShapeDtypeStruct((bsz, t_len, A_WIDTH), BF16),
        jax.ShapeDtypeStruct((bsz, t_len, A_WIDTH), BF16),
        jax.ShapeDtypeStruct((bsz, t_len, A_WIDTH), BF16),
        jax.ShapeDtypeStruct((bsz, t_len, LANES), F32),
        jax.ShapeDtypeStruct((bsz, B_WIDTH, t_len), BF16),
        jax.ShapeDtypeStruct((bsz, t_len, B_KV_WIDTH), BF16),
        jax.ShapeDtypeStruct((bsz, B_KV_WIDTH, t_len), BF16),
        jax.ShapeDtypeStruct((bsz, t_len, B_WIDTH), BF16),
    ]
    out_specs = [tok(A_WIDTH), tok(A_WIDTH), tok(A_WIDTH), tok(A_WIDTH), tok(LANES),
                 tok_t(B_WIDTH), tok(B_KV_WIDTH), tok_t(B_KV_WIDTH), tok(B_WIDTH)]
    return pl.pallas_call(
        _inproj_kernel,
        grid=grid,
        in_specs=[tok(d), wa_spec, full(wb), full(wt), full(cw), full(alog_vec), full(dtb_vec)],
        out_specs=out_specs,
        out_shape=out_shapes,
        scratch_shapes=[
            pltpu.VMEM((3, tm + SUBLANES, A_WIDTH), F32),
            pltpu.VMEM((3, SUBLANES, A_WIDTH), F32),
        ],
        compiler_params=pltpu.CompilerParams(
            dimension_semantics=("arbitrary", "arbitrary"),
            vmem_limit_bytes=VMEM_LIMIT_BYTES),
        name="inproj",
    )(x, w_in_all, wb, wt, cw, alog_vec, dtb_vec)


def _split3(x):
    hi = x.astype(BF16)
    r1 = x - hi.astype(F32)
    mid = r1.astype(BF16)
    lo = (r1 - mid.astype(F32)).astype(BF16)
    return hi, mid, lo


def _delta_kernel(q_ref, k_ref, v_ref, bg_ref, nw_ref, o_ref, s_ref):
    tb = q_ref.shape[1]
    n_chunks = tb // CHUNK
    t = pl.program_id(1)

    @pl.when(t == 0)
    def _():
        s_ref[...] = jnp.zeros_like(s_ref)

    bg = bg_ref[0]
    r_i = lax.broadcasted_iota(jnp.int32, (tb, tb), 0)
    c_i = lax.broadcasted_iota(jnp.int32, (tb, tb), 1)
    same_chunk = (c_i & -CHUNK) == (r_i & -CHUNK)
    tril_blk = jnp.where((c_i <= r_i) & same_chunk, 1.0, 0.0).astype(BF16)
    hi, mid, lo = _split3(bg)
    gcum_all = _dot(tril_blk, hi) + _dot(tril_blk, mid) + _dot(tril_blk, lo)

    row = lax.broadcasted_iota(jnp.int32, (CHUNK, LANES), 0)
    lane = lax.broadcasted_iota(jnp.int32, (CHUNK, LANES), 1)
    col = jnp.where(lane >= CHUNK, lane - CHUNK, lane)
    causal = col <= row
    strict = col < row
    lo_half = lane < CHUNK
    eye_lo = jnp.where((lane == row), 1.0, 0.0)
    zeros_pad = jnp.zeros((CHUNK, LANES), F32)
    nw = nw_ref[...]

    pairs = [(c, hd) for c in range(n_chunks) for hd in range(A_HEADS)]
    csl = lambda c: slice(c * CHUNK, (c + 1) * CHUNK)
    hsl = lambda hd: slice(hd * A_HEAD_DIM, (hd + 1) * A_HEAD_DIM)
    gcums = [gcum_all[csl(c), :] for c in range(n_chunks)]
    g2ts = [jnp.concatenate([g, g], axis=0).T for g in gcums]

    st = {}
    for (c, hd) in pairs:
        gcum = gcums[c]
        q = q_ref[0, csl(c), hsl(hd)]
        k = k_ref[0, csl(c), hsl(hd)]
        gcol = gcum[:, A_HEADS + hd:A_HEADS + hd + 1]
        grow = g2ts[c][A_HEADS + hd:A_HEADS + hd + 1, :]
        bcol = bg[csl(c), hd:hd + 1]
        glast = gcum[CHUNK - 1:CHUNK, A_HEADS + hd:A_HEADS + hd + 1]
        decay = jnp.where(causal, jnp.exp(jnp.minimum(gcol - grow, 0.0)), 0.0)
        k2 = jnp.concatenate([k, k], axis=0)
        qk_kk = _dot_nt(jnp.concatenate([q, k], axis=0), k2)
        qk = qk_kk[:CHUNK] * decay
        n_mat = jnp.where(strict, -(qk_kk[CHUNK:] * decay * bcol), 0.0)
        st[(c, hd)] = dict(q=q, k=k, gcol=gcol, bcol=bcol, glast=glast, qk=qk,
                           tp=jnp.where(lo_half, eye_lo, n_mat))

    for _ in range(6):
        for p in pairs:
            tp = st[p]["tp"]
            lhs = jnp.where(lo_half, 0.0, tp).astype(BF16)
            tpb = tp.astype(BF16)
            r = _dot(lhs, jnp.concatenate([tpb, tpb], axis=0))
            st[p]["tp"] = jnp.where(lo_half, tp, 0.0) + r

    for p in pairs:
        d = st[p]
        t_lhs = jnp.where(lo_half, d["tp"], 0.0).astype(BF16)
        kf = d["k"].astype(F32)
        vf = v_ref[0, csl(p[0]), hsl(p[1])].astype(F32)
        eg = jnp.exp(d["gcol"])
        x_cat = jnp.concatenate([vf * d["bcol"], kf * (d["bcol"] * eg)], axis=1).astype(BF16)
        uw = _dot(t_lhs, jnp.concatenate([x_cat, jnp.zeros_like(x_cat)], axis=0))
        uwb = uw.astype(BF16)
        d["uw2"] = jnp.concatenate([uwb, jnp.zeros_like(uwb)], axis=0)
        d["eg"] = eg
        kd = kf * jnp.exp(d["glast"] - d["gcol"])
        d["kdt"] = jnp.concatenate([kd, zeros_pad], axis=0).T.astype(BF16)

    for p in pairs:
        d = st[p]
        d["cm_mm"] = _dot(d["kdt"], d["uw2"])
        qk_lhs = jnp.where(lo_half, d["qk"], 0.0).astype(BF16)
        oi_qw = _dot(qk_lhs, d["uw2"])
        d["oi"] = oi_qw[:, :A_HEAD_DIM]
        d["qp"] = d["q"].astype(F32) * d["eg"] - oi_qw[:, A_HEAD_DIM:]

    for (c, hd) in pairs:
        d = st[(c, hd)]
        s_old = s_ref[hd]
        lhs_seq = jnp.concatenate([d["cm_mm"][:, A_HEAD_DIM:], d["qp"]], axis=0).astype(BF16)
        r_seq = _dot(lhs_seq, s_old.astype(BF16))
        s_ref[hd] = jnp.exp(d["glast"]) * s_old + d["cm_mm"][:, :A_HEAD_DIM] - r_seq[:A_HEAD_DIM]
        o = r_seq[A_HEAD_DIM:] + d["oi"]
        ms = jnp.mean(o * o, axis=-1, keepdims=True)
        o_ref[0, csl(c), hsl(hd)] = (o * lax.rsqrt(ms + RMS_EPS) * nw).astype(o_ref.dtype)


def _delta(qa, ka, va, bg, nw):
    bsz, t_len, _ = qa.shape
    tb = TB_DELTA
    grid = (bsz, t_len // tb)
    tok = lambda width: pl.BlockSpec((1, tb, width), lambda b, t: (b, t, 0))
    return pl.pallas_call(
        _delta_kernel,
        grid=grid,
        in_specs=[tok(A_WIDTH), tok(A_WIDTH), tok(A_WIDTH), tok(LANES),
                  pl.BlockSpec(nw.shape, lambda b, t: (0, 0))],
        out_specs=tok(A_WIDTH),
        out_shape=jax.ShapeDtypeStruct((bsz, t_len, A_WIDTH), BF16),
        scratch_shapes=[pltpu.VMEM((A_HEADS, A_HEAD_DIM, A_HEAD_DIM), F32)],
        compiler_params=pltpu.CompilerParams(
            dimension_semantics=("arbitrary", "arbitrary"),
            vmem_limit_bytes=VMEM_LIMIT_BYTES),
        name="delta",
    )(qa, ka, va, bg, nw)


def _swa_kernel(qt_ref, kp_ref, kc_ref, vtp_ref, vtc_ref, slope_ref, sink_ref, o_ref):
    n = pl.program_id(1)
    nqb = qt_ref.shape[2] // BLOCK
    band = 2 * BLOCK
    hq = B_Q_HEADS * BLOCK
    k_all = jnp.concatenate([kp_ref[0], kc_ref[0]], axis=0)
    vt_all = jnp.concatenate([vtp_ref[0], vtc_ref[0]], axis=1)

    s_idx = lax.broadcasted_iota(jnp.int32, (band, BLOCK), 0)
    q_idx = lax.broadcasted_iota(jnp.int32, (band, BLOCK), 1)
    dist = q_idx + BLOCK - s_idx
    valid = (dist >= 0) & (dist < WINDOW)
    dist_f = dist.astype(F32)
    hconst = lambda ref, h, shape: jnp.broadcast_to(ref[h:h + 1, 0:1], shape)
    bias = [jnp.where(valid, -hconst(slope_ref, h, (band, BLOCK)) * dist_f, -jnp.inf)
            for h in range(B_Q_HEADS)]
    sink = [hconst(sink_ref, h, (1, BLOCK)) for h in range(B_Q_HEADS)]
    no_prev = s_idx < BLOCK
    zq = jnp.zeros((B_HEAD_DIM, B_GROUP * BLOCK), BF16)

    scores = []
    for i in range(nqb):
        qs = slice(i * BLOCK, (i + 1) * BLOCK)
        heads = [qt_ref[0, h * B_HEAD_DIM:(h + 1) * B_HEAD_DIM, qs] for h in range(B_Q_HEADS)]
        q_rhs = jnp.concatenate(
            [jnp.concatenate(heads[:B_GROUP] + [zq], axis=1),
             jnp.concatenate([zq] + heads[B_GROUP:], axis=1)], axis=0)
        scores.append(_dot(k_all[i * BLOCK:i * BLOCK + band], q_rhs))

    probs, invs = [], []
    for i in range(nqb):
        p_heads, inv_heads = [], []
        for h in range(B_Q_HEADS):
            s = scores[i][:, h * BLOCK:(h + 1) * BLOCK] + bias[h]
            if i == 0:
                s = jnp.where(no_prev & (n == 0), -jnp.inf, s)
            m = jnp.maximum(jnp.max(s, axis=0, keepdims=True), sink[h])
            p = jnp.exp(s - m)
            den = jnp.sum(p, axis=0, keepdims=True) + jnp.exp(sink[h] - m)
            p_heads.append(p.astype(BF16))
            inv_heads.append(1.0 / den)
        probs.append(jnp.concatenate(p_heads, axis=1))
        invs.append(inv_heads)

    outs = [_dot(vt_all[:, i * BLOCK:i * BLOCK + band], probs[i]) for i in range(nqb)]

    for i in range(nqb):
        qs = slice(i * BLOCK, (i + 1) * BLOCK)
        for j in range(B_Q_HEADS // 2):
            r0 = (2 * j // B_GROUP) * B_HEAD_DIM
            pair_t = jnp.concatenate(
                [outs[i][r0:r0 + B_HEAD_DIM, (2 * j + c) * BLOCK:(2 * j + c + 1) * BLOCK] * invs[i][2 * j + c]
                 for c in range(2)], axis=0)
            o_ref[0, qs, j * LANES:(j + 1) * LANES] = pair_t.T.astype(o_ref.dtype)


def _swa(qbt, kb, vbt, slope_rows, sink_rows):
    bsz, _, t_len = qbt.shape
    nqb = QB_SWA
    tq = nqb * BLOCK
    grid = (bsz, t_len // tq)
    prev_blk = lambda n: jnp.maximum(n * nqb - 1, 0)
    full = lambda a: pl.BlockSpec(a.shape, lambda b, n: (0,) * a.ndim)
    return pl.pallas_call(
        _swa_kernel,
        grid=grid,
        in_specs=[pl.BlockSpec((1, B_WIDTH, tq), lambda b, n: (b, 0, n)),
                  pl.BlockSpec((1, BLOCK, B_KV_WIDTH), lambda b, n: (b, prev_blk(n), 0)),
                  pl.BlockSpec((1, tq, B_KV_WIDTH), lambda b, n: (b, n, 0)),
                  pl.BlockSpec((1, B_KV_WIDTH, BLOCK), lambda b, n: (b, 0, prev_blk(n))),
                  pl.BlockSpec((1, B_KV_WIDTH, tq), lambda b, n: (b, 0, n)),
                  full(slope_rows), full(sink_rows)],
        out_specs=pl.BlockSpec((1, tq, B_WIDTH), lambda b, n: (b, n, 0)),
        out_shape=jax.ShapeDtypeStruct((bsz, t_len, B_WIDTH), BF16),
        compiler_params=pltpu.CompilerParams(
            dimension_semantics=("arbitrary", "arbitrary"),
            vmem_limit_bytes=VMEM_LIMIT_BYTES),
        name="swa",
    )(qbt, kb, kb, vbt, vbt, slope_rows, sink_rows)


def _outproj_kernel(alpha, x_ref, ya_ref, za_ref, yb_ref, zb_ref, w_ref, g_ref, b_ref, o_ref):
    ya = (ya_ref[0].astype(F32) * _silu(za_ref[0].astype(F32))).astype(BF16)
    yb = (yb_ref[0].astype(F32) * _silu(zb_ref[0].astype(F32))).astype(BF16)
    y = _dot(ya, w_ref[:A_WIDTH, :]) + _dot(yb, w_ref[A_WIDTH:, :])
    r = alpha * x_ref[0] + y
    mu = jnp.mean(r, axis=-1, keepdims=True)
    rc = r - mu
    var = jnp.mean(rc * rc, axis=-1, keepdims=True)
    o_ref[0] = rc * lax.rsqrt(var + LN_EPS) * g_ref[...] + b_ref[...]


def _outproj(x, ya, za, yb, zb, w_out_all, layer, ln_g, ln_b, alpha):
    bsz, t_len, d = x.shape
    tm = TM_OUT
    grid = (bsz, t_len // tm)
    tok = lambda width: pl.BlockSpec((1, tm, width), lambda b, t: (b, t, 0))
    full = lambda a: pl.BlockSpec(a.shape, lambda b, t: (0,) * a.ndim)
    w_spec = pl.BlockSpec((None,) + w_out_all.shape[1:], lambda b, t: (layer, 0, 0))
    return pl.pallas_call(
        functools.partial(_outproj_kernel, alpha),
        grid=grid,
        in_specs=[tok(d), tok(A_WIDTH), tok(A_WIDTH), tok(B_WIDTH), tok(B_WIDTH), w_spec,
                  full(ln_g), full(ln_b)],
        out_specs=tok(d),
        out_shape=jax.ShapeDtypeStruct((bsz, t_len, d), F32),
        compiler_params=pltpu.CompilerParams(
            dimension_semantics=("arbitrary", "arbitrary"),
            vmem_limit_bytes=VMEM_LIMIT_BYTES),
        name="outproj",
    )(x, ya, za, yb, zb, w_out_all, ln_g, ln_b)


def _lane_rows(vals):
    return jnp.broadcast_to(vals.astype(F32)[:, None], (vals.shape[0], LANES))


def kernel(x, w_in, conv_w, a_log, dt_bias, norm_w, sinks, w_out, ln_g, ln_b):
    depth, d, _ = w_in.shape
    alpha = (2 * depth) ** 0.25
    w_in_bf = w_in.astype(BF16)
    w_out_bf = w_out.astype(BF16)
    o_small = COLS_A
    o_qb = o_small + 2 * A_HEADS
    o_kb = o_qb + B_WIDTH
    o_vb = o_kb + B_KV_WIDTH
    o_zb = o_vb + B_KV_WIDTH
    wb_all = jnp.concatenate(
        [w_in_bf[:, :, o_kb:o_vb], w_in_bf[:, :, o_zb:], w_in_bf[:, :, o_small:o_qb],
         jnp.zeros((depth, d, LANES - 2 * A_HEADS), BF16)], axis=2)
    wt_all = jnp.swapaxes(
        jnp.concatenate([w_in_bf[:, :, o_qb:o_kb], w_in_bf[:, :, o_vb:o_zb]], axis=2), 1, 2)
    pad = jnp.zeros((LANES - 2 * A_HEADS,), F32)
    zeros_h = jnp.zeros((A_HEADS,), F32)
    slope_rows = _lane_rows(
        jnp.asarray([2.0 ** (-8.0 * (h + 1) / B_Q_HEADS) for h in range(B_Q_HEADS)], F32))
    for l in range(depth):
        alog_vec = jnp.concatenate([zeros_h, a_log[l].astype(F32), pad])[None, :]
        dtb_vec = jnp.concatenate([zeros_h, dt_bias[l].astype(F32), pad])[None, :]
        qa, ka, va, za, bg, qbt, kb, vbt, zb = _inproj(
            x, w_in_bf, l, wb_all[l], wt_all[l], conv_w[l].astype(F32), alog_vec, dtb_vec)
        ya = _delta(qa, ka, va, bg, norm_w[l].astype(F32)[None, :])
        yb = _swa(qbt, kb, vbt, slope_rows, _lane_rows(sinks[l]))
        x = _outproj(x, ya, za, yb, zb, w_out_bf, l, ln_g[l].astype(F32)[None, :],
                     ln_b[l].astype(F32)[None, :], alpha)
    return x
```

```python
import functools

import jax
import jax.numpy as jnp
from jax import lax
from jax.experimental import pallas as pl
from jax.experimental.pallas import tpu as pltpu

F32 = jnp.float32
BF16 = jnp.bfloat16

LANES = 128
SUBLANES = 8
VMEM_LIMIT_BYTES = 48 * 1024 * 1024

A_HEADS = 4
A_HEAD_DIM = 128
A_WIDTH = A_HEADS * A_HEAD_DIM
CONV_K = 4
CHUNK = 64
B_Q_HEADS = 8
B_KV_HEADS = 2
B_HEAD_DIM = 64
B_GROUP = B_Q_HEADS // B_KV_HEADS
B_WIDTH = B_Q_HEADS * B_HEAD_DIM
B_KV_WIDTH = B_KV_HEADS * B_HEAD_DIM
WINDOW = 128
BLOCK = 128
LN_EPS = 1e-5
RMS_EPS = 1e-6
L2_EPS = 1e-6

COLS_A = 4 * A_WIDTH
COL_KB = 0
COL_ZB = COL_KB + B_KV_WIDTH
COL_SMALL = COL_ZB + B_WIDTH
COLS_B = COL_SMALL + LANES

TM_PROJ = 512
CONV_ROWS = 32
TM_OUT = 1024
TB_DELTA = 512
N_GROUPS = 8
SKEW = 1
QB_SWA = 8


def _silu(x):
    return x * (1.0 / (1.0 + jnp.exp(-x)))


def _sigmoid(x):
    return 1.0 / (1.0 + jnp.exp(-x))


def _softplus(x):
    return jnp.maximum(x, 0.0) + jnp.log(1.0 + jnp.exp(-jnp.abs(x)))


def _dot(a, b):
    return jnp.dot(a, b, preferred_element_type=F32)


def _dot_nt(a, b):
    return lax.dot_general(a, b, (((1,), (1,)), ((), ())), preferred_element_type=F32)


def _inproj_kernel(x_ref, wa_ref, wb_ref, wt_ref, cw_ref, alog_ref, dtb_ref,
                   qa_ref, ka_ref, va_ref, za_ref, bg_ref, qbt_ref, kb_ref, vbt_ref, zb_ref,
                   ext_ref, prev_ref):
    tm = x_ref.shape[1]
    t = pl.program_id(1)

    @pl.when(t == 0)
    def _():
        prev_ref[...] = jnp.zeros_like(prev_ref)

    xb = x_ref[0].astype(BF16)

    def proj_a(c0, n):
        return _dot(xb, wa_ref[:, c0:c0 + n])

    def proj_b(c0, n):
        return _dot(xb, wb_ref[:, c0:c0 + n])

    def conv_group(gi, h, out_ref):
        c0 = gi * A_WIDTH
        ext_ref[gi, 0:SUBLANES, :] = prev_ref[gi]
        ext_ref[gi, SUBLANES:SUBLANES + tm, :] = h
        prev_ref[gi] = ext_ref[gi, tm:tm + SUBLANES, :]
        scale = A_HEAD_DIM ** -0.5 if gi == 0 else 1.0
        for hd in range(A_HEADS):
            sl = slice(hd * A_HEAD_DIM, (hd + 1) * A_HEAD_DIM)
            wsl = slice(c0 + hd * A_HEAD_DIM, c0 + (hd + 1) * A_HEAD_DIM)
            taps = [cw_ref[CONV_K - 1 - s:CONV_K - s, wsl] for s in range(CONV_K)]
            for r0 in range(0, tm, CONV_ROWS):
                acc = None
                for s in range(CONV_K):
                    term = ext_ref[gi, SUBLANES + r0 - s:SUBLANES + r0 - s + CONV_ROWS, sl] * taps[s]
                    acc = term if acc is None else acc + term
                a = _silu(acc)
                if gi < 2:
                    ss = jnp.sum(a * a, axis=-1, keepdims=True)
                    a = a * (lax.rsqrt(ss + L2_EPS) * scale)
                out_ref[0, r0:r0 + CONV_ROWS, sl] = a.astype(out_ref.dtype)

    h_q = proj_a(0 * A_WIDTH, A_WIDTH)
    h_za = proj_a(3 * A_WIDTH, A_WIDTH)
    conv_group(0, h_q, qa_ref)
    h_k = proj_a(1 * A_WIDTH, A_WIDTH)
    h_t = _dot_nt(wt_ref[...], xb)
    za_ref[0] = h_za.astype(za_ref.dtype)
    conv_group(1, h_k, ka_ref)
    h_v = proj_a(2 * A_WIDTH, A_WIDTH)
    h_zb = proj_b(COL_ZB, B_WIDTH)
    qbt_ref[0] = (h_t[:B_WIDTH] * (B_HEAD_DIM ** -0.5)).astype(qbt_ref.dtype)
    vbt_ref[0] = h_t[B_WIDTH:].astype(vbt_ref.dtype)
    conv_group(2, h_v, va_ref)
    h_kb = proj_b(COL_KB, B_KV_WIDTH)
    hs = proj_b(COL_SMALL, LANES)
    zb_ref[0] = h_zb.astype(zb_ref.dtype)
    kb_ref[0] = h_kb.astype(kb_ref.dtype)

    lane = lax.broadcasted_iota(jnp.int32, hs.shape, 1)
    beta = _sigmoid(hs)
    g = -jnp.exp(alog_ref[...]) * _softplus(hs + dtb_ref[...])
    bg_ref[0] = jnp.where(lane < A_HEADS, beta, g)


def _inproj(x, w_in_all, layer, wb, wt, cw, alog_vec, dtb_vec):
    bsz, t_len, d = x.shape
    tm = TM_PROJ
    grid = (bsz, t_len // tm)
    tok = lambda width: pl.BlockSpec((1, tm, width), lambda b, t: (b, t, 0))
    tok_t = lambda rows: pl.BlockSpec((1, rows, tm), lambda b, t: (b, 0, t))
    full = lambda a: pl.BlockSpec(a.shape, lambda b, t: (0,) * a.ndim)
    wa_spec = pl.BlockSpec((None, d, COLS_A), lambda b, t: (layer, 0, 0))
    out_shapes = [
        jax.ShapeDtypeStruct((bsz, t_len, A_WIDTH), BF16),
        jax.ShapeDtypeStruct((bsz, t_len, A_WIDTH), BF16),
        jax.ShapeDtypeStruct((bsz, t_len, A_WIDTH), BF16),
        jax.ShapeDtypeStruct((bsz, t_len, A_WIDTH), BF16),
        jax.ShapeDtypeStruct((bsz, t_len, LANES), F32),
        jax.ShapeDtypeStruct((bsz, B_WIDTH, t_len), BF16),
        jax.ShapeDtypeStruct((bsz, t_len, B_KV_WIDTH), BF16),
        jax.ShapeDtypeStruct((bsz, B_KV_WIDTH, t_len), BF16),
        jax.ShapeDtypeStruct((bsz, t_len, B_WIDTH), BF16),
    ]
    out_specs = [tok(A_WIDTH), tok(A_WIDTH), tok(A_WIDTH), tok(A_WIDTH), tok(LANES),
                 tok_t(B_WIDTH), tok(B_KV_WIDTH), tok_t(B_KV_WIDTH), tok(B_WIDTH)]
    return pl.pallas_call(
        _inproj_kernel,
        grid=grid,
        in_specs=[tok(d), wa_spec, full(wb), full(wt), full(cw), full(alog_vec), full(dtb_vec)],
        out_specs=out_specs,
        out_shape=out_shapes,
        scratch_shapes=[
            pltpu.VMEM((3, tm + SUBLANES, A_WIDTH), F32),
            pltpu.VMEM((3, SUBLANES, A_WIDTH), F32),
        ],
        compiler_params=pltpu.CompilerParams(
            dimension_semantics=("arbitrary", "arbitrary"),
            vmem_limit_bytes=VMEM_LIMIT_BYTES),
        name="inproj",
    )(x, w_in_all, wb, wt, cw, alog_vec, dtb_vec)


def _split3(x):
    hi = x.astype(BF16)
    r1 = x - hi.astype(F32)
    mid = r1.astype(BF16)
    lo = (r1 - mid.astype(F32)).astype(BF16)
    return hi, mid, lo


def _delta_kernel(q_ref, k_ref, v_ref, bg_ref, nw_ref, o_ref, s_ref):
    tb = q_ref.shape[1]
    n_chunks = tb // CHUNK
    t = pl.program_id(1)

    @pl.when(t == 0)
    def _():
        s_ref[...] = jnp.zeros_like(s_ref)

    bg = bg_ref[0]
    r_i = lax.broadcasted_iota(jnp.int32, (CHUNK, CHUNK), 0)
    c_i = lax.broadcasted_iota(jnp.int32, (CHUNK, CHUNK), 1)
    tril = jnp.where(c_i <= r_i, 1.0, 0.0).astype(BF16)
    bg_cat = jnp.concatenate([bg[c * CHUNK:(c + 1) * CHUNK] for c in range(n_chunks)], axis=1)
    hi, mid, lo = _split3(bg_cat)
    gcum_cat = _dot(tril, hi) + _dot(tril, mid) + _dot(tril, lo)

    row = lax.broadcasted_iota(jnp.int32, (CHUNK, LANES), 0)
    lane = lax.broadcasted_iota(jnp.int32, (CHUNK, LANES), 1)
    col = jnp.where(lane >= CHUNK, lane - CHUNK, lane)
    causal = col <= row
    strict = col < row
    lo_half = lane < CHUNK
    lo_row = lo_half[0:1]
    eye2 = jnp.where(col == row, 1.0, 0.0)
    nw = nw_ref[...]
    half = LANES // 2

    csl = lambda c: slice(c * CHUNK, (c + 1) * CHUNK)
    hsl = lambda hd: slice(hd * A_HEAD_DIM, (hd + 1) * A_HEAD_DIM)
    gcums = [gcum_cat[:, c * LANES:(c + 1) * LANES] for c in range(n_chunks)]
    g2ts = [jnp.concatenate([g, g], axis=0).T for g in gcums]
    gidx = lambda hd: slice(A_HEADS + hd, A_HEADS + hd + 1)

    st = {}
    chains = {}

    def stage_products(c, m):
        ha, hb = 2 * m, 2 * m + 1
        gcum = gcums[c]
        qs = [q_ref[0, csl(c), hsl(h)] for h in (ha, hb)]
        ks = [k_ref[0, csl(c), hsl(h)] for h in (ha, hb)]
        gcols = [gcum[:, gidx(h)] for h in (ha, hb)]
        bcols = [bg[csl(c), h:h + 1] for h in (ha, hb)]
        glasts = [gcum[CHUNK - 1:CHUNK, gidx(h)] for h in (ha, hb)]
        gcol_p = jnp.where(lo_half, gcols[0], gcols[1])
        grow_p = jnp.where(lo_row, g2ts[c][gidx(ha), :], g2ts[c][gidx(hb), :])
        bcol_p = jnp.where(lo_half, bcols[0], bcols[1])
        decay = jnp.where(causal, jnp.exp(jnp.minimum(gcol_p - grow_p, 0.0)), 0.0)
        kt = jnp.concatenate(ks, axis=0).astype(F32).T
        prod = _dot(jnp.concatenate([qs[0], ks[0], qs[1], ks[1]], axis=0), kt.astype(BF16))
        qk = jnp.where(lo_half, prod[0:CHUNK], prod[2 * CHUNK:3 * CHUNK]) * decay
        kk = jnp.where(lo_half, prod[CHUNK:2 * CHUNK], prod[3 * CHUNK:])
        glast_row = jnp.where(lo_row, glasts[0], glasts[1])
        st[(c, m)] = dict(qs=qs, ks=ks, gcols=gcols, bcols=bcols, glasts=glasts, qk=qk,
                          kdt=kt * jnp.exp(glast_row - grow_p),
                          t=eye2, p=jnp.where(strict, -(kk * decay * bcol_p), 0.0))

    n_levels = 6

    def stage_double(pr, lvl):
        d = st[pr]
        pb = d["p"].astype(BF16)
        tb16 = d["t"].astype(BF16)
        zero = jnp.zeros_like(pb)
        if lvl < n_levels - 1:
            rhs = jnp.concatenate(
                [jnp.concatenate([jnp.where(lo_half, tb16, zero), jnp.where(lo_half, pb, zero)], axis=1),
                 jnp.concatenate([jnp.where(lo_half, zero, tb16), jnp.where(lo_half, zero, pb)], axis=1)],
                axis=0)
            r = _dot(pb, rhs)
            d["t"] = d["t"] + r[:, :LANES]
            d["p"] = r[:, LANES:]
        else:
            rhs = jnp.concatenate([jnp.where(lo_half, tb16, zero), jnp.where(lo_half, zero, tb16)], axis=0)
            d["t"] = d["t"] + _dot(pb, rhs)

    def stage_uw(c, m):
        d = st[(c, m)]
        t_sw = pltpu.roll(d["t"], half, axis=1)
        kdt_sw = pltpu.roll(d["kdt"], half, axis=1)
        qk_sw = pltpu.roll(d["qk"], half, axis=1)
        for i, (t_x, kdt_x, qk_x) in enumerate(((d["t"], d["kdt"], d["qk"]), (t_sw, kdt_sw, qk_sw))):
            hd = 2 * m + i
            kf = d["ks"][i].astype(F32)
            vf = v_ref[0, csl(c), hsl(hd)].astype(F32)
            bcol, gcol, glast = d["bcols"][i], d["gcols"][i], d["glasts"][i]
            eg = jnp.exp(gcol)
            x_cat = jnp.concatenate([vf * bcol, kf * (bcol * eg)], axis=1).astype(BF16)
            uw = _dot(t_x[:, :half].astype(BF16), x_cat)
            lhs = jnp.concatenate([kdt_x[:, :half], qk_x[:, :half]], axis=0).astype(BF16)
            chains[(c, hd)] = dict(uw=uw.astype(BF16), lhs=lhs, eg=eg, glast=glast, q=d["qs"][i])

    def stage_cm(key):
        d = chains[key]
        r = _dot(d["lhs"], d["uw"])
        d["cm"] = r[:A_HEAD_DIM, :A_HEAD_DIM]
        d["mm"] = r[:A_HEAD_DIM, A_HEAD_DIM:]
        d["oi"] = r[A_HEAD_DIM:, :A_HEAD_DIM]
        d["qp"] = d["q"].astype(F32) * d["eg"] - r[A_HEAD_DIM:, A_HEAD_DIM:]

    def stage_seq(c):
        for hd in range(A_HEADS):
            d = chains[(c, hd)]
            s_old = s_ref[hd]
            lhs_seq = jnp.concatenate([d["mm"], d["qp"]], axis=0).astype(BF16)
            r_seq = _dot(lhs_seq, s_old.astype(BF16))
            s_ref[hd] = jnp.exp(d["glast"]) * s_old + d["cm"] - r_seq[:A_HEAD_DIM]
            o = r_seq[A_HEAD_DIM:] + d["oi"]
            ms = jnp.mean(o * o, axis=-1, keepdims=True)
            o_ref[0, csl(c), hsl(hd)] = (o * lax.rsqrt(ms + RMS_EPS) * nw).astype(o_ref.dtype)

    group = n_chunks // N_GROUPS
    group_slots = []
    for g0 in range(0, n_chunks, group):
        gp = [(c, m) for c in range(g0, g0 + group) for m in range(A_HEADS // 2)]
        slots = [lambda gp=gp: [stage_products(c, m) for (c, m) in gp]]
        for lvl in range(n_levels):
            slots.append(lambda gp=gp, lvl=lvl: [stage_double(pr, lvl) for pr in gp])
        slots.append(lambda gp=gp: [stage_uw(c, m) for (c, m) in gp])
        slots.append(lambda gp=gp: [stage_cm((c, 2 * m + i)) for (c, m) in gp for i in range(2)])
        for c in range(g0, g0 + group):
            slots.append(lambda c=c: stage_seq(c))
        group_slots.append(slots)
    n_slots = len(group_slots[0])
    for tick in range(n_slots + SKEW * (N_GROUPS - 1)):
        for g, slots in enumerate(group_slots):
            k = tick - g * SKEW
            if 0 <= k < n_slots:
                slots[k]()


def _delta(qa, ka, va, bg, nw):
    bsz, t_len, _ = qa.shape
    tb = TB_DELTA
    grid = (bsz, t_len // tb)
    tok = lambda width: pl.BlockSpec((1, tb, width), lambda b, t: (b, t, 0))
    return pl.pallas_call(
        _delta_kernel,
        grid=grid,
        in_specs=[tok(A_WIDTH), tok(A_WIDTH), tok(A_WIDTH), tok(LANES),
                  pl.BlockSpec(nw.shape, lambda b, t: (0, 0))],
        out_specs=tok(A_WIDTH),
        out_shape=jax.ShapeDtypeStruct((bsz, t_len, A_WIDTH), BF16),
        scratch_shapes=[pltpu.VMEM((A_HEADS, A_HEAD_DIM, A_HEAD_DIM), F32)],
        compiler_params=pltpu.CompilerParams(
            dimension_semantics=("arbitrary", "arbitrary"),
            vmem_limit_bytes=VMEM_LIMIT_BYTES),
        name="delta",
    )(qa, ka, va, bg, nw)


def _swa_kernel(qt_ref, kp_ref, kc_ref, vtp_ref, vtc_ref, slope_ref, sink_ref, o_ref):
    n = pl.program_id(1)
    nqb = qt_ref.shape[2] // BLOCK
    band = 2 * BLOCK
    hq = B_Q_HEADS * BLOCK
    k_all = jnp.concatenate([kp_ref[0], kc_ref[0]], axis=0)
    vt_all = jnp.concatenate([vtp_ref[0], vtc_ref[0]], axis=1)

    s_idx = lax.broadcasted_iota(jnp.int32, (band, BLOCK), 0)
    q_idx = lax.broadcasted_iota(jnp.int32, (band, BLOCK), 1)
    dist = q_idx + BLOCK - s_idx
    valid = (dist >= 0) & (dist < WINDOW)
    dist_f = dist.astype(F32)
    hconst = lambda ref, h, shape: jnp.broadcast_to(ref[h:h + 1, 0:1], shape)
    bias = [jnp.where(valid, -hconst(slope_ref, h, (band, BLOCK)) * dist_f, -jnp.inf)
            for h in range(B_Q_HEADS)]
    sink = [hconst(sink_ref, h, (1, BLOCK)) for h in range(B_Q_HEADS)]
    no_prev = s_idx < BLOCK
    zq = jnp.zeros((B_HEAD_DIM, B_GROUP * BLOCK), BF16)

    scores = []
    for i in range(nqb):
        qs = slice(i * BLOCK, (i + 1) * BLOCK)
        heads = [qt_ref[0, h * B_HEAD_DIM:(h + 1) * B_HEAD_DIM, qs] for h in range(B_Q_HEADS)]
        q_rhs = jnp.concatenate(
            [jnp.concatenate(heads[:B_GROUP] + [zq], axis=1),
             jnp.concatenate([zq] + heads[B_GROUP:], axis=1)], axis=0)
        scores.append(_dot(k_all[i * BLOCK:i * BLOCK + band], q_rhs))

    probs, invs = [], []
    for i in range(nqb):
        p_heads, inv_heads = [], []
        for h in range(B_Q_HEADS):
            s = scores[i][:, h * BLOCK:(h + 1) * BLOCK] + bias[h]
            if i == 0:
                s = jnp.where(no_prev & (n == 0), -jnp.inf, s)
            m = jnp.maximum(jnp.max(s, axis=0, keepdims=True), sink[h])
            p = jnp.exp(s - m)
            den = jnp.sum(p, axis=0, keepdims=True) + jnp.exp(sink[h] - m)
            p_heads.append(p.astype(BF16))
            inv_heads.append(1.0 / den)
        probs.append(jnp.concatenate(p_heads, axis=1))
        invs.append(inv_heads)

    outs = [_dot(vt_all[:, i * BLOCK:i * BLOCK + band], probs[i]) for i in range(nqb)]

    for i in range(nqb):
        qs = slice(i * BLOCK, (i + 1) * BLOCK)
        for j in range(B_Q_HEADS // 2):
            r0 = (2 * j // B_GROUP) * B_HEAD_DIM
            pair_t = jnp.concatenate(
                [outs[i][r0:r0 + B_HEAD_DIM, (2 * j + c) * BLOCK:(2 * j + c + 1) * BLOCK] * invs[i][2 * j + c]
                 for c in range(2)], axis=0)
            o_ref[0, qs, j * LANES:(j + 1) * LANES] = pair_t.T.astype(o_ref.dtype)


def _swa(qbt, kb, vbt, slope_rows, sink_rows):
    bsz, _, t_len = qbt.shape
    nqb = QB_SWA
    tq = nqb * BLOCK
    grid = (bsz, t_len // tq)
    prev_blk = lambda n: jnp.maximum(n * nqb - 1, 0)
    full = lambda a: pl.BlockSpec(a.shape, lambda b, n: (0,) * a.ndim)
    return pl.pallas_call(
        _swa_kernel,
        grid=grid,
        in_specs=[pl.BlockSpec((1, B_WIDTH, tq), lambda b, n: (b, 0, n)),
                  pl.BlockSpec((1, BLOCK, B_KV_WIDTH), lambda b, n: (b, prev_blk(n), 0)),
                  pl.BlockSpec((1, tq, B_KV_WIDTH), lambda b, n: (b, n, 0)),
                  pl.BlockSpec((1, B_KV_WIDTH, BLOCK), lambda b, n: (b, 0, prev_blk(n))),
                  pl.BlockSpec((1, B_KV_WIDTH, tq), lambda b, n: (b, 0, n)),
                  full(slope_rows), full(sink_rows)],
        out_specs=pl.BlockSpec((1, tq, B_WIDTH), lambda b, n: (b, n, 0)),
        out_shape=jax.ShapeDtypeStruct((bsz, t_len, B_WIDTH), BF16),
        compiler_params=pltpu.CompilerParams(
            dimension_semantics=("arbitrary", "arbitrary"),
            vmem_limit_bytes=VMEM_LIMIT_BYTES),
        name="swa",
    )(qbt, kb, kb, vbt, vbt, slope_rows, sink_rows)


def _outproj_kernel(alpha, x_ref, ya_ref, za_ref, yb_ref, zb_ref, w_ref, g_ref, b_ref, o_ref):
    ya = (ya_ref[0].astype(F32) * _silu(za_ref[0].astype(F32))).astype(BF16)
    yb = (yb_ref[0].astype(F32) * _silu(zb_ref[0].astype(F32))).astype(BF16)
    y = _dot(ya, w_ref[:A_WIDTH, :]) + _dot(yb, w_ref[A_WIDTH:, :])
    r = alpha * x_ref[0] + y
    mu = jnp.mean(r, axis=-1, keepdims=True)
    rc = r - mu
    var = jnp.mean(rc * rc, axis=-1, keepdims=True)
    o_ref[0] = rc * lax.rsqrt(var + LN_EPS) * g_ref[...] + b_ref[...]


def _outproj(x, ya, za, yb, zb, w_out_all, layer, ln_g, ln_b, alpha):
    bsz, t_len, d = x.shape
    tm = TM_OUT
    grid = (bsz, t_len // tm)
    tok = lambda width: pl.BlockSpec((1, tm, width), lambda b, t: (b, t, 0))
    full = lambda a: pl.BlockSpec(a.shape, lambda b, t: (0,) * a.ndim)
    w_spec = pl.BlockSpec((None,) + w_out_all.shape[1:], lambda b, t: (layer, 0, 0))
    return pl.pallas_call(
        functools.partial(_outproj_kernel, alpha),
        grid=grid,
        in_specs=[tok(d), tok(A_WIDTH), tok(A_WIDTH), tok(B_WIDTH), tok(B_WIDTH), w_spec,
                  full(ln_g), full(ln_b)],
        out_specs=tok(d),
        out_shape=jax.ShapeDtypeStruct((bsz, t_len, d), F32),
        compiler_params=pltpu.CompilerParams(
            dimension_semantics=("arbitrary", "arbitrary"),
            vmem_limit_bytes=VMEM_LIMIT_BYTES),
        name="outproj",
    )(x, ya, za, yb, zb, w_out_all, ln_g, ln_b)


def _lane_rows(vals):
    return jnp.broadcast_to(vals.astype(F32)[:, None], (vals.shape[0], LANES))


def kernel(x, w_in, conv_w, a_log, dt_bias, norm_w, sinks, w_out, ln_g, ln_b):
    depth, d, _ = w_in.shape
    alpha = (2 * depth) ** 0.25
    w_in_bf = w_in.astype(BF16)
    w_out_bf = w_out.astype(BF16)
    o_small = COLS_A
    o_qb = o_small + 2 * A_HEADS
    o_kb = o_qb + B_WIDTH
    o_vb = o_kb + B_KV_WIDTH
    o_zb = o_vb + B_KV_WIDTH
    wb_all = jnp.concatenate(
        [w_in_bf[:, :, o_kb:o_vb], w_in_bf[:, :, o_zb:], w_in_bf[:, :, o_small:o_qb],
         jnp.zeros((depth, d, LANES - 2 * A_HEADS), BF16)], axis=2)
    wt_all = jnp.swapaxes(
        jnp.concatenate([w_in_bf[:, :, o_qb:o_kb], w_in_bf[:, :, o_vb:o_zb]], axis=2), 1, 2)
    pad = jnp.zeros((LANES - 2 * A_HEADS,), F32)
    zeros_h = jnp.zeros((A_HEADS,), F32)
    slope_rows = _lane_rows(
        jnp.asarray([2.0 ** (-8.0 * (h + 1) / B_Q_HEADS) for h in range(B_Q_HEADS)], F32))
    for l in range(depth):
        alog_vec = jnp.concatenate([zeros_h, a_log[l].astype(F32), pad])[None, :]
        dtb_vec = jnp.concatenate([zeros_h, dt_bias[l].astype(F32), pad])[None, :]
        qa, ka, va, za, bg, qbt, kb, vbt, zb = _inproj(
            x, w_in_bf, l, wb_all[l], wt_all[l], conv_w[l].astype(F32), alog_vec, dtb_vec)
        ya = _delta(qa, ka, va, bg, norm_w[l].astype(F32)[None, :])
        yb = _swa(qbt, kb, vbt, slope_rows, _lane_rows(sinks[l]))
        x = _outproj(x, ya, za, yb, zb, w_out_bf, l, ln_g[l].astype(F32)[None, :],
                     ln_b[l].astype(F32)[None, :], alpha)
    return x
```

```python
import functools

import jax
import jax.numpy as jnp
from jax import lax
from jax.experimental import pallas as pl
from jax.experimental.pallas import tpu as pltpu

F32 = jnp.float32
BF16 = jnp.bfloat16

LANES = 128
SUBLANES = 8
VMEM_LIMIT_BYTES = 48 * 1024 * 1024

A_HEADS = 4
A_HEAD_DIM = 128
A_WIDTH = A_HEADS * A_HEAD_DIM
CONV_K = 4
CHUNK = 64
B_Q_HEADS = 8
B_KV_HEADS = 2
B_HEAD_DIM = 64
B_GROUP = B_Q_HEADS // B_KV_HEADS
B_WIDTH = B_Q_HEADS * B_HEAD_DIM
B_KV_WIDTH = B_KV_HEADS * B_HEAD_DIM
WINDOW = 128
BLOCK = 128
LN_EPS = 1e-5
RMS_EPS = 1e-6
L2_EPS = 1e-6

COLS_A = 4 * A_WIDTH
COL_KB = 0
COL_ZB = COL_KB + B_KV_WIDTH
COL_SMALL = COL_ZB + B_WIDTH
COLS_B = COL_SMALL + LANES

TM_PROJ = 512
PROJ_COLS = 256
TM_OUT = 1024
TB_DELTA = 1024
N_GROUPS = 16
SKEW = 1
QB_SWA = 8


def _silu(x):
    return x * (1.0 / (1.0 + jnp.exp(-x)))


def _sigmoid(x):
    return 1.0 / (1.0 + jnp.exp(-x))


def _softplus(x):
    return jnp.maximum(x, 0.0) + jnp.log(1.0 + jnp.exp(-jnp.abs(x)))


def _dot(a, b):
    return jnp.dot(a, b, preferred_element_type=F32)


def _dot_nt(a, b):
    return lax.dot_general(a, b, (((1,), (1,)), ((), ())), preferred_element_type=F32)


def _inproj_kernel(x_ref, wa_ref, wb_ref, wt_ref, cw_ref, alog_ref, dtb_ref,
                   qa_ref, ka_ref, va_ref, za_ref, bg_ref, qbt_ref, kb_ref, vbt_ref, zb_ref,
                   prev_ref, prevr_ref, scr_ref, scrt_ref):
    tm = x_ref.shape[1]
    t = pl.program_id(1)

    @pl.when(t == 0)
    def _():
        prev_ref[...] = jnp.zeros_like(prev_ref)
        prevr_ref[...] = jnp.zeros_like(prevr_ref)

    xb = x_ref[0].astype(BF16)

    for c0 in range(0, COLS_A, PROJ_COLS):
        scr_ref[:, c0:c0 + PROJ_COLS] = _dot(xb, wa_ref[:, c0:c0 + PROJ_COLS])
    scrt_ref[...] = _dot_nt(wt_ref[...], xb)
    for c0 in range(0, COLS_B, PROJ_COLS):
        scr_ref[:, COLS_A + c0:COLS_A + c0 + PROJ_COLS] = _dot(xb, wb_ref[:, c0:c0 + PROJ_COLS])

    row8 = lax.broadcasted_iota(jnp.int32, (SUBLANES, A_HEAD_DIM), 0)

    def shift_rows(x, carry, s):
        y = pltpu.roll(x, s, axis=0)
        top = jnp.where(row8 < s, pltpu.roll(carry, s, axis=0), y[0:SUBLANES])
        return jnp.concatenate([top, y[SUBLANES:]], axis=0)

    def conv_group(gi, out_ref):
        c0 = gi * A_WIDTH
        scale = A_HEAD_DIM ** -0.5 if gi == 0 else 1.0
        for hd in range(A_HEADS):
            sl = slice(hd * A_HEAD_DIM, (hd + 1) * A_HEAD_DIM)
            wsl = slice(c0 + hd * A_HEAD_DIM, c0 + (hd + 1) * A_HEAD_DIM)
            w0, w1, w2, w3 = [cw_ref[j:j + 1, wsl] for j in range(CONV_K)]
            hc = scr_ref[:, wsl]
            h1 = shift_rows(hc, prev_ref[gi, :, sl], 1)
            r = w1 * hc + w0 * h1
            acc = w3 * hc + w2 * h1 + shift_rows(r, prevr_ref[gi, :, sl], 2)
            prev_ref[gi, :, sl] = hc[tm - SUBLANES:tm]
            prevr_ref[gi, :, sl] = r[tm - SUBLANES:tm]
            a = _silu(acc)
            if gi < 2:
                ss = jnp.sum(a * a, axis=-1, keepdims=True)
                a = a * (lax.rsqrt(ss + L2_EPS) * scale)
            out_ref[0, :, sl] = a.astype(out_ref.dtype)

    conv_group(0, qa_ref)
    conv_group(1, ka_ref)
    conv_group(2, va_ref)
    za_ref[0] = scr_ref[:, 3 * A_WIDTH:4 * A_WIDTH].astype(za_ref.dtype)
    qbt_ref[0] = (scrt_ref[:B_WIDTH, :] * (B_HEAD_DIM ** -0.5)).astype(qbt_ref.dtype)
    vbt_ref[0] = scrt_ref[B_WIDTH:, :].astype(vbt_ref.dtype)
    kb_ref[0] = scr_ref[:, COLS_A + COL_KB:COLS_A + COL_KB + B_KV_WIDTH].astype(kb_ref.dtype)
    zb_ref[0] = scr_ref[:, COLS_A + COL_ZB:COLS_A + COL_ZB + B_WIDTH].astype(zb_ref.dtype)
    hs = scr_ref[:, COLS_A + COL_SMALL:COLS_A + COL_SMALL + LANES]

    lane = lax.broadcasted_iota(jnp.int32, hs.shape, 1)
    beta = _sigmoid(hs)
    g = -jnp.exp(alog_ref[...]) * _softplus(hs + dtb_ref[...])
    bg_ref[0] = jnp.where(lane < A_HEADS, beta, g)


def _inproj(x, w_in_all, layer, wb, wt, cw, alog_vec, dtb_vec):
    bsz, t_len, d = x.shape
    tm = TM_PROJ
    grid = (bsz, t_len // tm)
    tok = lambda width: pl.BlockSpec((1, tm, width), lambda b, t: (b, t, 0))
    tok_t = lambda rows: pl.BlockSpec((1, rows, tm), lambda b, t: (b, 0, t))
    full = lambda a: pl.BlockSpec(a.shape, lambda b, t: (0,) * a.ndim)
    wa_spec = pl.BlockSpec((None, d, COLS_A), lambda b, t: (layer, 0, 0))
    out_shapes = [
        jax.ShapeDtypeStruct((bsz, t_len, A_WIDTH), BF16),
        jax.ShapeDtypeStruct((bsz, t_len, A_WIDTH), BF16),
        jax.ShapeDtypeStruct((bsz, t_len, A_WIDTH), BF16),
        jax.ShapeDtypeStruct((bsz, t_len, A_WIDTH), BF16),
        jax.ShapeDtypeStruct((bsz, t_len, LANES), F32),
        jax.ShapeDtypeStruct((bsz, B_WIDTH, t_len), BF16),
        jax.ShapeDtypeStruct((bsz, t_len, B_KV_WIDTH), BF16),
        jax.ShapeDtypeStruct((bsz, B_KV_WIDTH, t_len), BF16),
        jax.ShapeDtypeStruct((bsz, t_len, B_WIDTH), BF16),
    ]
    out_specs = [tok(A_WIDTH), tok(A_WIDTH), tok(A_WIDTH), tok(A_WIDTH), tok(LANES),
                 tok_t(B_WIDTH), tok(B_KV_WIDTH), tok_t(B_KV_WIDTH), tok(B_WIDTH)]
    return pl.pallas_call(
        _inproj_kernel,
        grid=grid,
        in_specs=[tok(d), wa_spec, full(wb), full(wt), full(cw), full(alog_vec), full(dtb_vec)],
        out_specs=out_specs,
        out_shape=out_shapes,
        scratch_shapes=[
            pltpu.VMEM((3, SUBLANES, A_WIDTH), F32),
            pltpu.VMEM((3, SUBLANES, A_WIDTH), F32),
            pltpu.VMEM((tm, COLS_A + COLS_B), F32),
            pltpu.VMEM((B_WIDTH + B_KV_WIDTH, tm), F32),
        ],
        compiler_params=pltpu.CompilerParams(
            dimension_semantics=("arbitrary", "arbitrary"),
            vmem_limit_bytes=VMEM_LIMIT_BYTES),
        name="inproj",
    )(x, w_in_all, wb, wt, cw, alog_vec, dtb_vec)


def _split3(x):
    hi = x.astype(BF16)
    r1 = x - hi.astype(F32)
    mid = r1.astype(BF16)
    lo = (r1 - mid.astype(F32)).astype(BF16)
    return hi, mid, lo


def _delta_kernel(q_ref, k_ref, v_ref, bg_ref, nw_ref, o_ref, s_ref):
    tb = q_ref.shape[1]
    n_chunks = tb // CHUNK
    t = pl.program_id(1)

    @pl.when(t == 0)
    def _():
        s_ref[...] = jnp.zeros_like(s_ref)

    bg = bg_ref[0]
    r_i = lax.broadcasted_iota(jnp.int32, (CHUNK, CHUNK), 0)
    c_i = lax.broadcasted_iota(jnp.int32, (CHUNK, CHUNK), 1)
    tril = jnp.where(c_i <= r_i, 1.0, 0.0).astype(BF16)
    bg_cat = jnp.concatenate([bg[c * CHUNK:(c + 1) * CHUNK] for c in range(n_chunks)], axis=1)
    hi, mid, lo = _split3(bg_cat)
    gcum_cat = _dot(tril, hi) + _dot(tril, mid) + _dot(tril, lo)

    row = lax.broadcasted_iota(jnp.int32, (CHUNK, LANES), 0)
    lane = lax.broadcasted_iota(jnp.int32, (CHUNK, LANES), 1)
    col = jnp.where(lane >= CHUNK, lane - CHUNK, lane)
    causal = col <= row
    strict = col < row
    lo_half = lane < CHUNK
    lo_row = lo_half[0:1]
    eye2 = jnp.where(col == row, 1.0, 0.0)
    nw = nw_ref[...]
    half = LANES // 2

    csl = lambda c: slice(c * CHUNK, (c + 1) * CHUNK)
    hsl = lambda hd: slice(hd * A_HEAD_DIM, (hd + 1) * A_HEAD_DIM)
    gcums = [gcum_cat[:, c * LANES:(c + 1) * LANES] for c in range(n_chunks)]
    g2ts = [jnp.concatenate([g, g], axis=0).T for g in gcums]
    gidx = lambda hd: slice(A_HEADS + hd, A_HEADS + hd + 1)

    st = {}
    chains = {}

    def stage_products(c, m):
        ha, hb = 2 * m, 2 * m + 1
        gcum = gcums[c]
        qs = [q_ref[0, csl(c), hsl(h)] for h in (ha, hb)]
        ks = [k_ref[0, csl(c), hsl(h)] for h in (ha, hb)]
        gcols = [gcum[:, gidx(h)] for h in (ha, hb)]
        bcols = [bg[csl(c), h:h + 1] for h in (ha, hb)]
        glasts = [gcum[CHUNK - 1:CHUNK, gidx(h)] for h in (ha, hb)]
        gcol_p = jnp.where(lo_half, gcols[0], gcols[1])
        grow_p = jnp.where(lo_row, g2ts[c][gidx(ha), :], g2ts[c][gidx(hb), :])
        bcol_p = jnp.where(lo_half, bcols[0], bcols[1])
        decay = jnp.where(causal, jnp.exp(jnp.minimum(gcol_p - grow_p, 0.0)), 0.0)
        kt = jnp.concatenate(ks, axis=0).astype(F32).T
        prod = _dot(jnp.concatenate([qs[0], ks[0], qs[1], ks[1]], axis=0), kt.astype(BF16))
        qk = jnp.where(lo_half, prod[0:CHUNK], prod[2 * CHUNK:3 * CHUNK]) * decay
        kk = jnp.where(lo_half, prod[CHUNK:2 * CHUNK], prod[3 * CHUNK:])
        glast_row = jnp.where(lo_row, glasts[0], glasts[1])
        st[(c, m)] = dict(qs=qs, ks=ks, gcols=gcols, bcols=bcols, glasts=glasts, qk=qk,
                          kdt=kt * jnp.exp(glast_row - grow_p),
                          t=eye2, p=jnp.where(strict, -(kk * decay * bcol_p), 0.0))

    n_levels = 6

    def stage_double(pr, lvl):
        d = st[pr]
        pb = d["p"].astype(BF16)
        tb16 = d["t"].astype(BF16)
        zero = jnp.zeros_like(pb)
        if lvl < n_levels - 1:
            rhs = jnp.concatenate(
                [jnp.concatenate([jnp.where(lo_half, tb16, zero), jnp.where(lo_half, pb, zero)], axis=1),
                 jnp.concatenate([jnp.where(lo_half, zero, tb16), jnp.where(lo_half, zero, pb)], axis=1)],
                axis=0)
            r = _dot(pb, rhs)
            d["t"] = d["t"] + r[:, :LANES]
            d["p"] = r[:, LANES:]
        else:
            rhs = jnp.concatenate([jnp.where(lo_half, tb16, zero), jnp.where(lo_half, zero, tb16)], axis=0)
            d["t"] = d["t"] + _dot(pb, rhs)

    def stage_uw(c, m):
        d = st[(c, m)]
        t_sw = pltpu.roll(d["t"], half, axis=1)
        kdt_sw = pltpu.roll(d["kdt"], half, axis=1)
        qk_sw = pltpu.roll(d["qk"], half, axis=1)
        for i, (t_x, kdt_x, qk_x) in enumerate(((d["t"], d["kdt"], d["qk"]), (t_sw, kdt_sw, qk_sw))):
            hd = 2 * m + i
            kf = d["ks"][i].astype(F32)
            vf = v_ref[0, csl(c), hsl(hd)].astype(F32)
            bcol, gcol, glast = d["bcols"][i], d["gcols"][i], d["glasts"][i]
            eg = jnp.exp(gcol)
            x_cat = jnp.concatenate([vf * bcol, kf * (bcol * eg)], axis=1).astype(BF16)
            uw = _dot(t_x[:, :half].astype(BF16), x_cat)
            lhs = jnp.concatenate([kdt_x[:, :half], qk_x[:, :half]], axis=0).astype(BF16)
            chains[(c, hd)] = dict(uw=uw.astype(BF16), lhs=lhs, eg=eg, glast=glast, q=d["qs"][i])

    def stage_cm(key):
        d = chains[key]
        r = _dot(d["lhs"], d["uw"])
        d["cm"] = r[:A_HEAD_DIM, :A_HEAD_DIM]
        d["mm"] = r[:A_HEAD_DIM, A_HEAD_DIM:]
        d["oi"] = r[A_HEAD_DIM:, :A_HEAD_DIM]
        d["qp"] = d["q"].astype(F32) * d["eg"] - r[A_HEAD_DIM:, A_HEAD_DIM:]

    def stage_seq(c):
        for hd in range(A_HEADS):
            d = chains[(c, hd)]
            s_old = s_ref[hd]
            lhs_seq = jnp.concatenate([d["mm"], d["qp"]], axis=0).astype(BF16)
            r_seq = _dot(lhs_seq, s_old.astype(BF16))
            s_ref[hd] = jnp.exp(d["glast"]) * s_old + d["cm"] - r_seq[:A_HEAD_DIM]
            o = r_seq[A_HEAD_DIM:] + d["oi"]
            ms = jnp.mean(o * o, axis=-1, keepdims=True)
            o_ref[0, csl(c), hsl(hd)] = (o * lax.rsqrt(ms + RMS_EPS) * nw).astype(o_ref.dtype)

    group = n_chunks // N_GROUPS
    group_slots = []
    for g0 in range(0, n_chunks, group):
        gp = [(c, m) for c in range(g0, g0 + group) for m in range(A_HEADS // 2)]
        slots = [lambda gp=gp: [stage_products(c, m) for (c, m) in gp]]
        for lvl in range(n_levels):
            slots.append(lambda gp=gp, lvl=lvl: [stage_double(pr, lvl) for pr in gp])
        slots.append(lambda gp=gp: [stage_uw(c, m) for (c, m) in gp])
        slots.append(lambda gp=gp: [stage_cm((c, 2 * m + i)) for (c, m) in gp for i in range(2)])
        for c in range(g0, g0 + group):
            slots.append(lambda c=c: stage_seq(c))
        group_slots.append(slots)
    n_slots = len(group_slots[0])
    for tick in range(n_slots + SKEW * (N_GROUPS - 1)):
        for g, slots in enumerate(group_slots):
            k = tick - g * SKEW
            if 0 <= k < n_slots:
                slots[k]()


def _delta(qa, ka, va, bg, nw):
    bsz, t_len, _ = qa.shape
    tb = TB_DELTA
    grid = (bsz, t_len // tb)
    tok = lambda width: pl.BlockSpec((1, tb, width), lambda b, t: (b, t, 0))
    return pl.pallas_call(
        _delta_kernel,
        grid=grid,
        in_specs=[tok(A_WIDTH), tok(A_WIDTH), tok(A_WIDTH), tok(LANES),
                  pl.BlockSpec(nw.shape, lambda b, t: (0, 0))],
        out_specs=tok(A_WIDTH),
        out_shape=jax.ShapeDtypeStruct((bsz, t_len, A_WIDTH), BF16),
        scratch_shapes=[pltpu.VMEM((A_HEADS, A_HEAD_DIM, A_HEAD_DIM), F32)],
        compiler_params=pltpu.CompilerParams(
            dimension_semantics=("arbitrary", "arbitrary"),
            vmem_limit_bytes=VMEM_LIMIT_BYTES),
        name="delta",
    )(qa, ka, va, bg, nw)


def _swa_kernel(qt_ref, kp_ref, kc_ref, vtp_ref, vtc_ref, slope_ref, sink_ref, o_ref):
    n = pl.program_id(1)
    nqb = qt_ref.shape[2] // BLOCK
    band = 2 * BLOCK
    hq = B_Q_HEADS * BLOCK
    k_all = jnp.concatenate([kp_ref[0], kc_ref[0]], axis=0)
    vt_all = jnp.concatenate([vtp_ref[0], vtc_ref[0]], axis=1)

    s_idx = lax.broadcasted_iota(jnp.int32, (band, BLOCK), 0)
    q_idx = lax.broadcasted_iota(jnp.int32, (band, BLOCK), 1)
    dist = q_idx + BLOCK - s_idx
    valid = (dist >= 0) & (dist < WINDOW)
    dist_f = dist.astype(F32)
    hconst = lambda ref, h, shape: jnp.broadcast_to(ref[h:h + 1, 0:1], shape)
    bias = [jnp.where(valid, -hconst(slope_ref, h, (band, BLOCK)) * dist_f, -jnp.inf)
            for h in range(B_Q_HEADS)]
    sink = [hconst(sink_ref, h, (1, BLOCK)) for h in range(B_Q_HEADS)]
    no_prev = s_idx < BLOCK
    zq = jnp.zeros((B_HEAD_DIM, B_GROUP * BLOCK), BF16)

    scores = []
    for i in range(nqb):
        qs = slice(i * BLOCK, (i + 1) * BLOCK)
        heads = [qt_ref[0, h * B_HEAD_DIM:(h + 1) * B_HEAD_DIM, qs] for h in range(B_Q_HEADS)]
        q_rhs = jnp.concatenate(
            [jnp.concatenate(heads[:B_GROUP] + [zq], axis=1),
             jnp.concatenate([zq] + heads[B_GROUP:], axis=1)], axis=0)
        scores.append(_dot(k_all[i * BLOCK:i * BLOCK + band], q_rhs))

    probs, invs = [], []
    for i in range(nqb):
        p_heads, inv_heads = [], []
        for h in range(B_Q_HEADS):
            s = scores[i][:, h * BLOCK:(h + 1) * BLOCK] + bias[h]
            if i == 0:
                s = jnp.where(no_prev & (n == 0), -jnp.inf, s)
            m = jnp.maximum(jnp.max(s, axis=0, keepdims=True), sink[h])
            p = jnp.exp(s - m)
            den = jnp.sum(p, axis=0, keepdims=True) + jnp.exp(sink[h] - m)
            p_heads.append(p.astype(BF16))
            inv_heads.append(1.0 / den)
        probs.append(jnp.concatenate(p_heads, axis=1))
        invs.append(inv_heads)

    outs = [_dot(vt_all[:, i * BLOCK:i * BLOCK + band], probs[i]) for i in range(nqb)]

    for i in range(nqb):
        qs = slice(i * BLOCK, (i + 1) * BLOCK)
        for j in range(B_Q_HEADS // 2):
            r0 = (2 * j // B_GROUP) * B_HEAD_DIM
            pair_t = jnp.concatenate(
                [outs[i][r0:r0 + B_HEAD_DIM, (2 * j + c) * BLOCK:(2 * j + c + 1) * BLOCK] * invs[i][2 * j + c]
                 for c in range(2)], axis=0)
            o_ref[0, qs, j * LANES:(j + 1) * LANES] = pair_t.T.astype(o_ref.dtype)


def _swa(qbt, kb, vbt, slope_rows, sink_rows):
    bsz, _, t_len = qbt.shape
    nqb = QB_SWA
    tq = nqb * BLOCK
    grid = (bsz, t_len // tq)
    prev_blk = lambda n: jnp.maximum(n * nqb - 1, 0)
    full = lambda a: pl.BlockSpec(a.shape, lambda b, n: (0,) * a.ndim)
    return pl.pallas_call(
        _swa_kernel,
        grid=grid,
        in_specs=[pl.BlockSpec((1, B_WIDTH, tq), lambda b, n: (b, 0, n)),
                  pl.BlockSpec((1, BLOCK, B_KV_WIDTH), lambda b, n: (b, prev_blk(n), 0)),
                  pl.BlockSpec((1, tq, B_KV_WIDTH), lambda b, n: (b, n, 0)),
                  pl.BlockSpec((1, B_KV_WIDTH, BLOCK), lambda b, n: (b, 0, prev_blk(n))),
                  pl.BlockSpec((1, B_KV_WIDTH, tq), lambda b, n: (b, 0, n)),
                  full(slope_rows), full(sink_rows)],
        out_specs=pl.BlockSpec((1, tq, B_WIDTH), lambda b, n: (b, n, 0)),
        out_shape=jax.ShapeDtypeStruct((bsz, t_len, B_WIDTH), BF16),
        compiler_params=pltpu.CompilerParams(
            dimension_semantics=("arbitrary", "arbitrary"),
            vmem_limit_bytes=VMEM_LIMIT_BYTES),
        name="swa",
    )(qbt, kb, kb, vbt, vbt, slope_rows, sink_rows)


def _outproj_kernel(alpha, x_ref, ya_ref, za_ref, yb_ref, zb_ref, w_ref, g_ref, b_ref, o_ref):
    ya = (ya_ref[0].astype(F32) * _silu(za_ref[0].astype(F32))).astype(BF16)
    yb = (yb_ref[0].astype(F32) * _silu(zb_ref[0].astype(F32))).astype(BF16)
    y = _dot(ya, w_ref[:A_WIDTH, :]) + _dot(yb, w_ref[A_WIDTH:, :])
    r = alpha * x_ref[0] + y
    mu = jnp.mean(r, axis=-1, keepdims=True)
    rc = r - mu
    var = jnp.mean(rc * rc, axis=-1, keepdims=True)
    o_ref[0] = rc * lax.rsqrt(var + LN_EPS) * g_ref[...] + b_ref[...]


def _outproj(x, ya, za, yb, zb, w_out_all, layer, ln_g, ln_b, alpha):
    bsz, t_len, d = x.shape
    tm = TM_OUT
    grid = (bsz, t_len // tm)
    tok = lambda width: pl.BlockSpec((1, tm, width), lambda b, t: (b, t, 0))
    full = lambda a: pl.BlockSpec(a.shape, lambda b, t: (0,) * a.ndim)
    w_spec = pl.BlockSpec((None,) + w_out_all.shape[1:], lambda b, t: (layer, 0, 0))
    return pl.pallas_call(
        functools.partial(_outproj_kernel, alpha),
        grid=grid,
        in_specs=[tok(d), tok(A_WIDTH), tok(A_WIDTH), tok(B_WIDTH), tok(B_WIDTH), w_spec,
                  full(ln_g), full(ln_b)],
        out_specs=tok(d),
        out_shape=jax.ShapeDtypeStruct((bsz, t_len, d), F32),
        compiler_params=pltpu.CompilerParams(
            dimension_semantics=("arbitrary", "arbitrary"),
            vmem_limit_bytes=VMEM_LIMIT_BYTES),
        name="outproj",
    )(x, ya, za, yb, zb, w_out_all, ln_g, ln_b)


def _lane_rows(vals):
    return jnp.broadcast_to(vals.astype(F32)[:, None], (vals.shape[0], LANES))


def kernel(x, w_in, conv_w, a_log, dt_bias, norm_w, sinks, w_out, ln_g, ln_b):
    depth, d, _ = w_in.shape
    alpha = (2 * depth) ** 0.25
    w_in_bf = w_in.astype(BF16)
    w_out_bf = w_out.astype(BF16)
    o_small = COLS_A
    o_qb = o_small + 2 * A_HEADS
    o_kb = o_qb + B_WIDTH
    o_vb = o_kb + B_KV_WIDTH
    o_zb = o_vb + B_KV_WIDTH
    wb_all = jnp.concatenate(
        [w_in_bf[:, :, o_kb:o_vb], w_in_bf[:, :, o_zb:], w_in_bf[:, :, o_small:o_qb],
         jnp.zeros((depth, d, LANES - 2 * A_HEADS), BF16)], axis=2)
    wt_all = jnp.swapaxes(
        jnp.concatenate([w_in[:, :, o_qb:o_kb], w_in[:, :, o_vb:o_zb]], axis=2), 1, 2).astype(BF16)
    pad = jnp.zeros((LANES - 2 * A_HEADS,), F32)
    zeros_h = jnp.zeros((A_HEADS,), F32)
    slope_rows = _lane_rows(
        jnp.asarray([2.0 ** (-8.0 * (h + 1) / B_Q_HEADS) for h in range(B_Q_HEADS)], F32))
    for l in range(depth):
        alog_vec = jnp.concatenate([zeros_h, a_log[l].astype(F32), pad])[None, :]
        dtb_vec = jnp.concatenate([zeros_h, dt_bias[l].astype(F32), pad])[None, :]
        qa, ka, va, za, bg, qbt, kb, vbt, zb = _inproj(
            x, w_in_bf, l, wb_all[l], wt_all[l], conv_w[l].astype(F32), alog_vec, dtb_vec)
        ya = _delta(qa, ka, va, bg, norm_w[l].astype(F32)[None, :])
        yb = _swa(qbt, kb, vbt, slope_rows, _lane_rows(sinks[l]))
        x = _outproj(x, ya, za, yb, zb, w_out_bf, l, ln_g[l].astype(F32)[None, :],
                     ln_b[l].astype(F32)[None, :], alpha)
    return x
```

```python
import functools

import jax
import jax.numpy as jnp
from jax import lax
from jax.experimental import pallas as pl
from jax.experimental.pallas import tpu as pltpu

F32 = jnp.float32
BF16 = jnp.bfloat16

LANES = 128
SUBLANES = 8
VMEM_LIMIT_BYTES = 48 * 1024 * 1024

A_HEADS = 4
A_HEAD_DIM = 128
A_WIDTH = A_HEADS * A_HEAD_DIM
CONV_K = 4
CHUNK = 64
B_Q_HEADS = 8
B_KV_HEADS = 2
B_HEAD_DIM = 64
B_GROUP = B_Q_HEADS // B_KV_HEADS
B_WIDTH = B_Q_HEADS * B_HEAD_DIM
B_KV_WIDTH = B_KV_HEADS * B_HEAD_DIM
WINDOW = 128
BLOCK = 128
LN_EPS = 1e-5
RMS_EPS = 1e-6
L2_EPS = 1e-6
LOG2E = 1.4426950408889634

COLS_A = 4 * A_WIDTH
COL_KB = 0
COL_ZB = COL_KB + B_KV_WIDTH
COL_SMALL = COL_ZB + B_WIDTH
COLS_B = COL_SMALL + LANES

TM_PROJ = 512
PROJ_COLS = 256
TM_OUT = 1024
TB_DELTA = 1024
N_GROUPS = 16
SKEW = 1
QB_SWA = 8


def _silu(x):
    return x * (1.0 / (1.0 + jnp.exp(-x)))


def _sigmoid(x):
    return 1.0 / (1.0 + jnp.exp(-x))


def _softplus(x):
    return jnp.maximum(x, 0.0) + jnp.log(1.0 + jnp.exp(-jnp.abs(x)))


def _dot(a, b):
    return jnp.dot(a, b, preferred_element_type=F32)


def _dot_nt(a, b):
    return lax.dot_general(a, b, (((1,), (1,)), ((), ())), preferred_element_type=F32)


def _inproj_kernel(x_ref, wa_ref, wb_ref, wt_ref, cw_ref, alog_ref, dtb_ref,
                   qa_ref, ka_ref, va_ref, za_ref, bg_ref, qbt_ref, kb_ref, vbt_ref, zb_ref,
                   prev_ref, prevr_ref, scr_ref, scrt_ref):
    tm = x_ref.shape[1]
    t = pl.program_id(1)

    @pl.when(t == 0)
    def _():
        prev_ref[...] = jnp.zeros_like(prev_ref)
        prevr_ref[...] = jnp.zeros_like(prevr_ref)

    xb = x_ref[0].astype(BF16)

    for c0 in range(0, COLS_A, PROJ_COLS):
        scr_ref[:, c0:c0 + PROJ_COLS] = _dot(xb, wa_ref[:, c0:c0 + PROJ_COLS])
    scrt_ref[...] = _dot_nt(wt_ref[...], xb)
    for c0 in range(0, COLS_B, PROJ_COLS):
        scr_ref[:, COLS_A + c0:COLS_A + c0 + PROJ_COLS] = _dot(xb, wb_ref[:, c0:c0 + PROJ_COLS])

    row8 = lax.broadcasted_iota(jnp.int32, (SUBLANES, A_HEAD_DIM), 0)

    def shift_rows(x, carry, s):
        y = pltpu.roll(x, s, axis=0)
        top = jnp.where(row8 < s, pltpu.roll(carry, s, axis=0), y[0:SUBLANES])
        return jnp.concatenate([top, y[SUBLANES:]], axis=0)

    def conv_group(gi, out_ref):
        c0 = gi * A_WIDTH
        scale = A_HEAD_DIM ** -0.5 if gi == 0 else 1.0
        for hd in range(A_HEADS):
            sl = slice(hd * A_HEAD_DIM, (hd + 1) * A_HEAD_DIM)
            wsl = slice(c0 + hd * A_HEAD_DIM, c0 + (hd + 1) * A_HEAD_DIM)
            w0, w1, w2, w3 = [cw_ref[j:j + 1, wsl] for j in range(CONV_K)]
            hc = scr_ref[:, wsl]
            h1 = shift_rows(hc, prev_ref[gi, :, sl], 1)
            r = w1 * hc + w0 * h1
            acc = w3 * hc + w2 * h1 + shift_rows(r, prevr_ref[gi, :, sl], 2)
            prev_ref[gi, :, sl] = hc[tm - SUBLANES:tm]
            prevr_ref[gi, :, sl] = r[tm - SUBLANES:tm]
            a = _silu(acc)
            if gi < 2:
                ss = jnp.sum(a * a, axis=-1, keepdims=True)
                a = a * (lax.rsqrt(ss + L2_EPS) * scale)
            out_ref[0, :, sl] = a.astype(out_ref.dtype)

    conv_group(0, qa_ref)
    conv_group(1, ka_ref)
    conv_group(2, va_ref)
    za_ref[0] = scr_ref[:, 3 * A_WIDTH:4 * A_WIDTH].astype(za_ref.dtype)
    qbt_ref[0] = (scrt_ref[:B_WIDTH, :] * (B_HEAD_DIM ** -0.5 * LOG2E)).astype(qbt_ref.dtype)
    vbt_ref[0] = scrt_ref[B_WIDTH:, :].astype(vbt_ref.dtype)
    kb_ref[0] = scr_ref[:, COLS_A + COL_KB:COLS_A + COL_KB + B_KV_WIDTH].astype(kb_ref.dtype)
    zb_ref[0] = scr_ref[:, COLS_A + COL_ZB:COLS_A + COL_ZB + B_WIDTH].astype(zb_ref.dtype)
    hs = scr_ref[:, COLS_A + COL_SMALL:COLS_A + COL_SMALL + LANES]

    lane = lax.broadcasted_iota(jnp.int32, hs.shape, 1)
    beta = _sigmoid(hs)
    g = -jnp.exp(alog_ref[...]) * _softplus(hs + dtb_ref[...])
    bg_ref[0] = jnp.where(lane < A_HEADS, beta, g)


def _inproj(x, w_in_all, layer, wb, wt, cw, alog_vec, dtb_vec):
    bsz, t_len, d = x.shape
    tm = TM_PROJ
    grid = (bsz, t_len // tm)
    tok = lambda width: pl.BlockSpec((1, tm, width), lambda b, t: (b, t, 0))
    tok_t = lambda rows: pl.BlockSpec((1, rows, tm), lambda b, t: (b, 0, t))
    full = lambda a: pl.BlockSpec(a.shape, lambda b, t: (0,) * a.ndim)
    wa_spec = pl.BlockSpec((None, d, COLS_A), lambda b, t: (layer, 0, 0))
    out_shapes = [
        jax.ShapeDtypeStruct((bsz, t_len, A_WIDTH), BF16),
        jax.ShapeDtypeStruct((bsz, t_len, A_WIDTH), BF16),
        jax.ShapeDtypeStruct((bsz, t_len, A_WIDTH), BF16),
        jax.ShapeDtypeStruct((bsz, t_len, A_WIDTH), BF16),
        jax.ShapeDtypeStruct((bsz, t_len, LANES), F32),
        jax.ShapeDtypeStruct((bsz, B_WIDTH, t_len), BF16),
        jax.ShapeDtypeStruct((bsz, t_len, B_KV_WIDTH), BF16),
        jax.ShapeDtypeStruct((bsz, B_KV_WIDTH, t_len), BF16),
        jax.ShapeDtypeStruct((bsz, t_len, B_WIDTH), BF16),
    ]
    out_specs = [tok(A_WIDTH), tok(A_WIDTH), tok(A_WIDTH), tok(A_WIDTH), tok(LANES),
                 tok_t(B_WIDTH), tok(B_KV_WIDTH), tok_t(B_KV_WIDTH), tok(B_WIDTH)]
    return pl.pallas_call(
        _inproj_kernel,
        grid=grid,
        in_specs=[tok(d), wa_spec, full(wb), full(wt), full(cw), full(alog_vec), full(dtb_vec)],
        out_specs=out_specs,
        out_shape=out_shapes,
        scratch_shapes=[
            pltpu.VMEM((3, SUBLANES, A_WIDTH), F32),
            pltpu.VMEM((3, SUBLANES, A_WIDTH), F32),
            pltpu.VMEM((tm, COLS_A + COLS_B), F32),
            pltpu.VMEM((B_WIDTH + B_KV_WIDTH, tm), F32),
        ],
        compiler_params=pltpu.CompilerParams(
            dimension_semantics=("arbitrary", "arbitrary"),
            vmem_limit_bytes=VMEM_LIMIT_BYTES),
        name="inproj",
    )(x, w_in_all, wb, wt, cw, alog_vec, dtb_vec)


def _split3(x):
    hi = x.astype(BF16)
    r1 = x - hi.astype(F32)
    mid = r1.astype(BF16)
    lo = (r1 - mid.astype(F32)).astype(BF16)
    return hi, mid, lo


def _delta_kernel(q_ref, k_ref, v_ref, bg_ref, nw_ref, o_ref, s_ref):
    tb = q_ref.shape[1]
    n_chunks = tb // CHUNK
    t = pl.program_id(1)

    @pl.when(t == 0)
    def _():
        s_ref[...] = jnp.zeros_like(s_ref)

    bg = bg_ref[0]
    r_i = lax.broadcasted_iota(jnp.int32, (CHUNK, CHUNK), 0)
    c_i = lax.broadcasted_iota(jnp.int32, (CHUNK, CHUNK), 1)
    tril = jnp.where(c_i <= r_i, 1.0, 0.0).astype(BF16)
    bg_cat = jnp.concatenate([bg[c * CHUNK:(c + 1) * CHUNK] for c in range(n_chunks)], axis=1)
    hi, mid, lo = _split3(bg_cat)
    gcum_cat = _dot(tril, hi) + _dot(tril, mid) + _dot(tril, lo)

    row = lax.broadcasted_iota(jnp.int32, (CHUNK, LANES), 0)
    lane = lax.broadcasted_iota(jnp.int32, (CHUNK, LANES), 1)
    col = jnp.where(lane >= CHUNK, lane - CHUNK, lane)
    causal = col <= row
    strict = col < row
    lo_half = lane < CHUNK
    lo_row = lo_half[0:1]
    eye2 = jnp.where(col == row, 1.0, 0.0)
    nw = nw_ref[...]
    half = LANES // 2

    csl = lambda c: slice(c * CHUNK, (c + 1) * CHUNK)
    hsl = lambda hd: slice(hd * A_HEAD_DIM, (hd + 1) * A_HEAD_DIM)
    gcums = [gcum_cat[:, c * LANES:(c + 1) * LANES] for c in range(n_chunks)]
    g2ts = [jnp.concatenate([g, g], axis=0).T for g in gcums]
    gidx = lambda hd: slice(A_HEADS + hd, A_HEADS + hd + 1)

    st = {}
    chains = {}

    def stage_products(c, m):
        ha, hb = 2 * m, 2 * m + 1
        gcum = gcums[c]
        qs = [q_ref[0, csl(c), hsl(h)] for h in (ha, hb)]
        ks = [k_ref[0, csl(c), hsl(h)] for h in (ha, hb)]
        gcols = [gcum[:, gidx(h)] for h in (ha, hb)]
        bcols = [bg[csl(c), h:h + 1] for h in (ha, hb)]
        glasts = [gcum[CHUNK - 1:CHUNK, gidx(h)] for h in (ha, hb)]
        gcol_p = jnp.where(lo_half, gcols[0], gcols[1])
        grow_p = jnp.where(lo_row, g2ts[c][gidx(ha), :], g2ts[c][gidx(hb), :])
        bcol_p = jnp.where(lo_half, bcols[0], bcols[1])
        decay = jnp.where(causal, jnp.exp(jnp.minimum(gcol_p - grow_p, 0.0)), 0.0)
        kt = jnp.concatenate(ks, axis=0).astype(F32).T
        prod = _dot(jnp.concatenate([qs[0], ks[0], qs[1], ks[1]], axis=0), kt.astype(BF16))
        qk = jnp.where(lo_half, prod[0:CHUNK], prod[2 * CHUNK:3 * CHUNK]) * decay
        kk = jnp.where(lo_half, prod[CHUNK:2 * CHUNK], prod[3 * CHUNK:])
        glast_row = jnp.where(lo_row, glasts[0], glasts[1])
        st[(c, m)] = dict(qs=qs, ks=ks, gcols=gcols, bcols=bcols, glasts=glasts, qk=qk,
                          kdt=kt * jnp.exp(glast_row - grow_p),
                          t=eye2, p=jnp.where(strict, -(kk * decay * bcol_p), 0.0))

    n_levels = 6

    def stage_double(pr, lvl):
        d = st[pr]
        pb = d["p"].astype(BF16)
        tb16 = d["t"].astype(BF16)
        zero = jnp.zeros_like(pb)
        if lvl < n_levels - 1:
            rhs = jnp.concatenate(
                [jnp.concatenate([jnp.where(lo_half, tb16, zero), jnp.where(lo_half, pb, zero)], axis=1),
                 jnp.concatenate([jnp.where(lo_half, zero, tb16), jnp.where(lo_half, zero, pb)], axis=1)],
                axis=0)
            r = _dot(pb, rhs)
            d["t"] = d["t"] + r[:, :LANES]
            d["p"] = r[:, LANES:]
        else:
            rhs = jnp.concatenate([jnp.where(lo_half, tb16, zero), jnp.where(lo_half, zero, tb16)], axis=0)
            d["t"] = d["t"] + _dot(pb, rhs)

    def stage_uw(c, m):
        d = st[(c, m)]
        t_sw = pltpu.roll(d["t"], half, axis=1)
        kdt_sw = pltpu.roll(d["kdt"], half, axis=1)
        qk_sw = pltpu.roll(d["qk"], half, axis=1)
        for i, (t_x, kdt_x, qk_x) in enumerate(((d["t"], d["kdt"], d["qk"]), (t_sw, kdt_sw, qk_sw))):
            hd = 2 * m + i
            kf = d["ks"][i].astype(F32)
            vf = v_ref[0, csl(c), hsl(hd)].astype(F32)
            bcol, gcol, glast = d["bcols"][i], d["gcols"][i], d["glasts"][i]
            eg = jnp.exp(gcol)
            x_cat = jnp.concatenate([vf * bcol, kf * (bcol * eg)], axis=1).astype(BF16)
            uw = _dot(t_x[:, :half].astype(BF16), x_cat)
            lhs = jnp.concatenate([kdt_x[:, :half], qk_x[:, :half]], axis=0).astype(BF16)
            chains[(c, hd)] = dict(uw=uw.astype(BF16), lhs=lhs, eg=eg, glast=glast, q=d["qs"][i])

    def stage_cm(key):
        d = chains[key]
        r = _dot(d["lhs"], d["uw"])
        d["cm"] = r[:A_HEAD_DIM, :A_HEAD_DIM]
        d["mm"] = r[:A_HEAD_DIM, A_HEAD_DIM:]
        d["oi"] = r[A_HEAD_DIM:, :A_HEAD_DIM]
        d["qp"] = d["q"].astype(F32) * d["eg"] - r[A_HEAD_DIM:, A_HEAD_DIM:]

    def stage_seq(c):
        for hd in range(A_HEADS):
            d = chains[(c, hd)]
            s_old = s_ref[hd]
            lhs_seq = jnp.concatenate([d["mm"], d["qp"]], axis=0).astype(BF16)
            r_seq = _dot(lhs_seq, s_old.astype(BF16))
            s_ref[hd] = jnp.exp(d["glast"]) * s_old + d["cm"] - r_seq[:A_HEAD_DIM]
            o = r_seq[A_HEAD_DIM:] + d["oi"]
            ms = jnp.mean(o * o, axis=-1, keepdims=True)
            o_ref[0, csl(c), hsl(hd)] = (o * lax.rsqrt(ms + RMS_EPS) * nw).astype(o_ref.dtype)

    group = n_chunks // N_GROUPS
    group_slots = []
    for g0 in range(0, n_chunks, group):
        gp = [(c, m) for c in range(g0, g0 + group) for m in range(A_HEADS // 2)]
        slots = [lambda gp=gp: [stage_products(c, m) for (c, m) in gp]]
        for lvl in range(n_levels):
            slots.append(lambda gp=gp, lvl=lvl: [stage_double(pr, lvl) for pr in gp])
        slots.append(lambda gp=gp: [stage_uw(c, m) for (c, m) in gp])
        slots.append(lambda gp=gp: [stage_cm((c, 2 * m + i)) for (c, m) in gp for i in range(2)])
        for c in range(g0, g0 + group):
            slots.append(lambda c=c: stage_seq(c))
        group_slots.append(slots)
    n_slots = len(group_slots[0])
    for tick in range(n_slots + SKEW * (N_GROUPS - 1)):
        for g, slots in enumerate(group_slots):
            k = tick - g * SKEW
            if 0 <= k < n_slots:
                slots[k]()


def _delta(qa, ka, va, bg, nw):
    bsz, t_len, _ = qa.shape
    tb = TB_DELTA
    grid = (bsz, t_len // tb)
    tok = lambda width: pl.BlockSpec((1, tb, width), lambda b, t: (b, t, 0))
    return pl.pallas_call(
        _delta_kernel,
        grid=grid,
        in_specs=[tok(A_WIDTH), tok(A_WIDTH), tok(A_WIDTH), tok(LANES),
                  pl.BlockSpec(nw.shape, lambda b, t: (0, 0))],
        out_specs=tok(A_WIDTH),
        out_shape=jax.ShapeDtypeStruct((bsz, t_len, A_WIDTH), BF16),
        scratch_shapes=[pltpu.VMEM((A_HEADS, A_HEAD_DIM, A_HEAD_DIM), F32)],
        compiler_params=pltpu.CompilerParams(
            dimension_semantics=("arbitrary", "arbitrary"),
            vmem_limit_bytes=VMEM_LIMIT_BYTES),
        name="delta",
    )(qa, ka, va, bg, nw)


def _swa_kernel(qt_ref, kp_ref, kc_ref, vtp_ref, vtc_ref, slope_ref, sink_ref, o_ref):
    n = pl.program_id(1)
    nqb = qt_ref.shape[2] // BLOCK
    band = 2 * BLOCK
    hq = B_Q_HEADS * BLOCK
    k_all = jnp.concatenate([kp_ref[0], kc_ref[0]], axis=0)
    vt_all = jnp.concatenate([vtp_ref[0], vtc_ref[0]], axis=1)

    s_idx = lax.broadcasted_iota(jnp.int32, (band, BLOCK), 0)
    q_idx = lax.broadcasted_iota(jnp.int32, (band, BLOCK), 1)
    dist = q_idx + BLOCK - s_idx
    valid = (dist >= 0) & (dist < WINDOW)
    dist_f = dist.astype(F32)
    hconst = lambda ref, h, shape: jnp.broadcast_to(ref[h:h + 1, 0:1], shape)
    bias = [jnp.where(valid, -(hconst(slope_ref, h, (band, BLOCK)) * LOG2E) * dist_f, -jnp.inf)
            for h in range(B_Q_HEADS)]
    sink = [hconst(sink_ref, h, (1, BLOCK)) * LOG2E for h in range(B_Q_HEADS)]
    no_prev = s_idx < BLOCK
    zq = jnp.zeros((B_HEAD_DIM, B_GROUP * BLOCK), BF16)

    scores = []
    for i in range(nqb):
        qs = slice(i * BLOCK, (i + 1) * BLOCK)
        heads = [qt_ref[0, h * B_HEAD_DIM:(h + 1) * B_HEAD_DIM, qs] for h in range(B_Q_HEADS)]
        q_rhs = jnp.concatenate(
            [jnp.concatenate(heads[:B_GROUP] + [zq], axis=1),
             jnp.concatenate([zq] + heads[B_GROUP:], axis=1)], axis=0)
        scores.append(_dot(k_all[i * BLOCK:i * BLOCK + band], q_rhs))

    probs, invs = [], []
    for i in range(nqb):
        p_heads, inv_heads = [], []
        for h in range(B_Q_HEADS):
            s = scores[i][:, h * BLOCK:(h + 1) * BLOCK] + bias[h]
            if i == 0:
                s = jnp.where(no_prev & (n == 0), -jnp.inf, s)
            m = jnp.maximum(jnp.max(s, axis=0, keepdims=True), sink[h])
            p = jnp.exp2(s - m)
            den = jnp.sum(p, axis=0, keepdims=True) + jnp.exp2(sink[h] - m)
            p_heads.append(p.astype(BF16))
            inv_heads.append(1.0 / den)
        probs.append(jnp.concatenate(p_heads, axis=1))
        invs.append(inv_heads)

    outs = [_dot(vt_all[:, i * BLOCK:i * BLOCK + band], probs[i]) for i in range(nqb)]

    for i in range(nqb):
        qs = slice(i * BLOCK, (i + 1) * BLOCK)
        for j in range(B_Q_HEADS // 2):
            r0 = (2 * j // B_GROUP) * B_HEAD_DIM
            pair_t = jnp.concatenate(
                [outs[i][r0:r0 + B_HEAD_DIM, (2 * j + c) * BLOCK:(2 * j + c + 1) * BLOCK] * invs[i][2 * j + c]
                 for c in range(2)], axis=0)
            o_ref[0, qs, j * LANES:(j + 1) * LANES] = pair_t.T.astype(o_ref.dtype)


def _swa(qbt, kb, vbt, slope_rows, sink_rows):
    bsz, _, t_len = qbt.shape
    nqb = QB_SWA
    tq = nqb * BLOCK
    grid = (bsz, t_len // tq)
    prev_blk = lambda n: jnp.maximum(n * nqb - 1, 0)
    full = lambda a: pl.BlockSpec(a.shape, lambda b, n: (0,) * a.ndim)
    return pl.pallas_call(
        _swa_kernel,
        grid=grid,
        in_specs=[pl.BlockSpec((1, B_WIDTH, tq), lambda b, n: (b, 0, n)),
                  pl.BlockSpec((1, BLOCK, B_KV_WIDTH), lambda b, n: (b, prev_blk(n), 0)),
                  pl.BlockSpec((1, tq, B_KV_WIDTH), lambda b, n: (b, n, 0)),
                  pl.BlockSpec((1, B_KV_WIDTH, BLOCK), lambda b, n: (b, 0, prev_blk(n))),
                  pl.BlockSpec((1, B_KV_WIDTH, tq), lambda b, n: (b, 0, n)),
                  full(slope_rows), full(sink_rows)],
        out_specs=pl.BlockSpec((1, tq, B_WIDTH), lambda b, n: (b, n, 0)),
        out_shape=jax.ShapeDtypeStruct((bsz, t_len, B_WIDTH), BF16),
        compiler_params=pltpu.CompilerParams(
            dimension_semantics=("arbitrary", "arbitrary"),
            vmem_limit_bytes=VMEM_LIMIT_BYTES),
        name="swa",
    )(qbt, kb, kb, vbt, vbt, slope_rows, sink_rows)


def _outproj_kernel(alpha, x_ref, ya_ref, za_ref, yb_ref, zb_ref, w_ref, g_ref, b_ref, o_ref):
    ya = (ya_ref[0].astype(F32) * _silu(za_ref[0].astype(F32))).astype(BF16)
    yb = (yb_ref[0].astype(F32) * _silu(zb_ref[0].astype(F32))).astype(BF16)
    y = _dot(ya, w_ref[:A_WIDTH, :]) + _dot(yb, w_ref[A_WIDTH:, :])
    r = alpha * x_ref[0] + y
    mu = jnp.mean(r, axis=-1, keepdims=True)
    rc = r - mu
    var = jnp.mean(rc * rc, axis=-1, keepdims=True)
    o_ref[0] = rc * lax.rsqrt(var + LN_EPS) * g_ref[...] + b_ref[...]


def _outproj(x, ya, za, yb, zb, w_out_all, layer, ln_g, ln_b, alpha):
    bsz, t_len, d = x.shape
    tm = TM_OUT
    grid = (bsz, t_len // tm)
    tok = lambda width: pl.BlockSpec((1, tm, width), lambda b, t: (b, t, 0))
    full = lambda a: pl.BlockSpec(a.shape, lambda b, t: (0,) * a.ndim)
    w_spec = pl.BlockSpec((None,) + w_out_all.shape[1:], lambda b, t: (layer, 0, 0))
    return pl.pallas_call(
        functools.partial(_outproj_kernel, alpha),
        grid=grid,
        in_specs=[tok(d), tok(A_WIDTH), tok(A_WIDTH), tok(B_WIDTH), tok(B_WIDTH), w_spec,
                  full(ln_g), full(ln_b)],
        out_specs=tok(d),
        out_shape=jax.ShapeDtypeStruct((bsz, t_len, d), F32),
        compiler_params=pltpu.CompilerParams(
            dimension_semantics=("arbitrary", "arbitrary"),
            vmem_limit_bytes=VMEM_LIMIT_BYTES),
        name="outproj",
    )(x, ya, za, yb, zb, w_out_all, ln_g, ln_b)


def _transpose_kernel(w_ref, o_ref):
    o_ref[0] = w_ref[0].T.astype(o_ref.dtype)


def _transpose_weights(w):
    depth, d, n = w.shape
    return pl.pallas_call(
        _transpose_kernel,
        grid=(depth,),
        in_specs=[pl.BlockSpec((1, d, n), lambda l: (l, 0, 0))],
        out_specs=pl.BlockSpec((1, n, d), lambda l: (l, 0, 0)),
        out_shape=jax.ShapeDtypeStruct((depth, n, d), BF16),
        compiler_params=pltpu.CompilerParams(
            dimension_semantics=("arbitrary",), vmem_limit_bytes=VMEM_LIMIT_BYTES),
        name="wtranspose",
    )(w)


def _lane_rows(vals):
    return jnp.broadcast_to(vals.astype(F32)[:, None], (vals.shape[0], LANES))


def kernel(x, w_in, conv_w, a_log, dt_bias, norm_w, sinks, w_out, ln_g, ln_b):
    depth, d, _ = w_in.shape
    alpha = (2 * depth) ** 0.25
    w_in_bf = w_in.astype(BF16)
    w_out_bf = w_out.astype(BF16)
    o_small = COLS_A
    o_qb = o_small + 2 * A_HEADS
    o_kb = o_qb + B_WIDTH
    o_vb = o_kb + B_KV_WIDTH
    o_zb = o_vb + B_KV_WIDTH
    wb_all = jnp.concatenate(
        [w_in_bf[:, :, o_kb:o_vb], w_in_bf[:, :, o_zb:], w_in_bf[:, :, o_small:o_qb],
         jnp.zeros((depth, d, LANES - 2 * A_HEADS), BF16)], axis=2)
    wt_all = _transpose_weights(
        jnp.concatenate([w_in[:, :, o_qb:o_kb], w_in[:, :, o_vb:o_zb]], axis=2))
    pad = jnp.zeros((LANES - 2 * A_HEADS,), F32)
    zeros_h = jnp.zeros((A_HEADS,), F32)
    slope_rows = _lane_rows(
        jnp.asarray([2.0 ** (-8.0 * (h + 1) / B_Q_HEADS) for h in range(B_Q_HEADS)], F32))
    for l in range(depth):
        alog_vec = jnp.concatenate([zeros_h, a_log[l].astype(F32), pad])[None, :]
        dtb_vec = jnp.concatenate([zeros_h, dt_bias[l].astype(F32), pad])[None, :]
        qa, ka, va, za, bg, qbt, kb, vbt, zb = _inproj(
            x, w_in_bf, l, wb_all[l], wt_all[l], conv_w[l].astype(F32), alog_vec, dtb_vec)
        ya = _delta(qa, ka, va, bg, norm_w[l].astype(F32)[None, :])
        yb = _swa(qbt, kb, vbt, slope_rows, _lane_rows(sinks[l]))
        x = _outproj(x, ya, za, yb, zb, w_out_bf, l, ln_g[l].astype(F32)[None, :],
                     ln_b[l].astype(F32)[None, :], alpha)
    return x
```

```python
import functools

import jax
import jax.numpy as jnp
from jax import lax
from jax.experimental import pallas as pl
from jax.experimental.pallas import tpu as pltpu

F32 = jnp.float32
BF16 = jnp.bfloat16

LANES = 128
SUBLANES = 8
VMEM_LIMIT_BYTES = 48 * 1024 * 1024

A_HEADS = 4
A_HEAD_DIM = 128
A_WIDTH = A_HEADS * A_HEAD_DIM
CONV_K = 4
CHUNK = 64
B_Q_HEADS = 8
B_KV_HEADS = 2
B_HEAD_DIM = 64
B_GROUP = B_Q_HEADS // B_KV_HEADS
B_WIDTH = B_Q_HEADS * B_HEAD_DIM
B_KV_WIDTH = B_KV_HEADS * B_HEAD_DIM
WINDOW = 128
BLOCK = 128
LN_EPS = 1e-5
RMS_EPS = 1e-6
L2_EPS = 1e-6
LOG2E = 1.4426950408889634

COLS_A = 4 * A_WIDTH
COL_KB = 0
COL_ZB = COL_KB + B_KV_WIDTH
COL_SMALL = COL_ZB + B_WIDTH
COLS_B = COL_SMALL + LANES

TM_PROJ = 512
PROJ_COLS = 256
TM_OUT = 1024
TB_DELTA = 2048
N_GROUPS = 32
SKEW = 1
QB_SWA = 8


def _silu(x):
    return x * (1.0 / (1.0 + jnp.exp(-x)))


def _sigmoid(x):
    return 1.0 / (1.0 + jnp.exp(-x))


def _softplus(x):
    return jnp.maximum(x, 0.0) + jnp.log(1.0 + jnp.exp(-jnp.abs(x)))


def _dot(a, b):
    return jnp.dot(a, b, preferred_element_type=F32)


def _dot_nt(a, b):
    return lax.dot_general(a, b, (((1,), (1,)), ((), ())), preferred_element_type=F32)


def _inproj_kernel(x_ref, wa_ref, wb_ref, wt_ref, cw_ref, alog_ref, dtb_ref,
                   qa_ref, ka_ref, va_ref, za_ref, bg_ref, qbt_ref, kb_ref, vbt_ref, zb_ref,
                   prev_ref, prevr_ref, scr_ref, scrt_ref):
    tm = x_ref.shape[1]
    t = pl.program_id(1)

    @pl.when(t == 0)
    def _():
        prev_ref[...] = jnp.zeros_like(prev_ref)
        prevr_ref[...] = jnp.zeros_like(prevr_ref)

    xb = x_ref[0].astype(BF16)

    for c0 in range(0, COLS_A, PROJ_COLS):
        scr_ref[:, c0:c0 + PROJ_COLS] = _dot(xb, wa_ref[:, c0:c0 + PROJ_COLS])
    scrt_ref[...] = _dot_nt(wt_ref[...], xb)
    for c0 in range(0, COLS_B, PROJ_COLS):
        scr_ref[:, COLS_A + c0:COLS_A + c0 + PROJ_COLS] = _dot(xb, wb_ref[:, c0:c0 + PROJ_COLS])

    row8 = lax.broadcasted_iota(jnp.int32, (SUBLANES, A_HEAD_DIM), 0)

    def shift_rows(x, carry, s):
        y = pltpu.roll(x, s, axis=0)
        top = jnp.where(row8 < s, pltpu.roll(carry, s, axis=0), y[0:SUBLANES])
        return jnp.concatenate([top, y[SUBLANES:]], axis=0)

    def conv_group(gi, out_ref):
        c0 = gi * A_WIDTH
        scale = A_HEAD_DIM ** -0.5 if gi == 0 else 1.0
        for hd in range(A_HEADS):
            sl = slice(hd * A_HEAD_DIM, (hd + 1) * A_HEAD_DIM)
            wsl = slice(c0 + hd * A_HEAD_DIM, c0 + (hd + 1) * A_HEAD_DIM)
            w0, w1, w2, w3 = [cw_ref[j:j + 1, wsl] for j in range(CONV_K)]
            hc = scr_ref[:, wsl]
            h1 = shift_rows(hc, prev_ref[gi, :, sl], 1)
            r = w1 * hc + w0 * h1
            acc = w3 * hc + w2 * h1 + shift_rows(r, prevr_ref[gi, :, sl], 2)
            prev_ref[gi, :, sl] = hc[tm - SUBLANES:tm]
            prevr_ref[gi, :, sl] = r[tm - SUBLANES:tm]
            a = _silu(acc)
            if gi < 2:
                ss = jnp.sum(a * a, axis=-1, keepdims=True)
                a = a * (lax.rsqrt(ss + L2_EPS) * scale)
            out_ref[0, :, sl] = a.astype(out_ref.dtype)

    conv_group(0, qa_ref)
    conv_group(1, ka_ref)
    conv_group(2, va_ref)
    za_ref[0] = scr_ref[:, 3 * A_WIDTH:4 * A_WIDTH].astype(za_ref.dtype)
    qbt_ref[0] = (scrt_ref[:B_WIDTH, :] * (B_HEAD_DIM ** -0.5 * LOG2E)).astype(qbt_ref.dtype)
    vbt_ref[0] = scrt_ref[B_WIDTH:, :].astype(vbt_ref.dtype)
    kb_ref[0] = scr_ref[:, COLS_A + COL_KB:COLS_A + COL_KB + B_KV_WIDTH].astype(kb_ref.dtype)
    zb_ref[0] = scr_ref[:, COLS_A + COL_ZB:COLS_A + COL_ZB + B_WIDTH].astype(zb_ref.dtype)
    hs = scr_ref[:, COLS_A + COL_SMALL:COLS_A + COL_SMALL + LANES]

    lane = lax.broadcasted_iota(jnp.int32, hs.shape, 1)
    beta = _sigmoid(hs)
    g = -jnp.exp(alog_ref[...]) * _softplus(hs + dtb_ref[...])
    bg_ref[0] = jnp.where(lane < A_HEADS, beta, g)


def _inproj(x, w_in_all, layer, wb, wt, cw, alog_vec, dtb_vec):
    bsz, t_len, d = x.shape
    tm = TM_PROJ
    grid = (bsz, t_len // tm)
    tok = lambda width: pl.BlockSpec((1, tm, width), lambda b, t: (b, t, 0))
    tok_t = lambda rows: pl.BlockSpec((1, rows, tm), lambda b, t: (b, 0, t))
    full = lambda a: pl.BlockSpec(a.shape, lambda b, t: (0,) * a.ndim)
    wa_spec = pl.BlockSpec((None, d, COLS_A), lambda b, t: (layer, 0, 0))
    out_shapes = [
        jax.ShapeDtypeStruct((bsz, t_len, A_WIDTH), BF16),
        jax.ShapeDtypeStruct((bsz, t_len, A_WIDTH), BF16),
        jax.ShapeDtypeStruct((bsz, t_len, A_WIDTH), BF16),
        jax.ShapeDtypeStruct((bsz, t_len, A_WIDTH), BF16),
        jax.ShapeDtypeStruct((bsz, t_len, LANES), F32),
        jax.ShapeDtypeStruct((bsz, B_WIDTH, t_len), BF16),
        jax.ShapeDtypeStruct((bsz, t_len, B_KV_WIDTH), BF16),
        jax.ShapeDtypeStruct((bsz, B_KV_WIDTH, t_len), BF16),
        jax.ShapeDtypeStruct((bsz, t_len, B_WIDTH), BF16),
    ]
    out_specs = [tok(A_WIDTH), tok(A_WIDTH), tok(A_WIDTH), tok(A_WIDTH), tok(LANES),
                 tok_t(B_WIDTH), tok(B_KV_WIDTH), tok_t(B_KV_WIDTH), tok(B_WIDTH)]
    return pl.pallas_call(
        _inproj_kernel,
        grid=grid,
        in_specs=[tok(d), wa_spec, full(wb), full(wt), full(cw), full(alog_vec), full(dtb_vec)],
        out_specs=out_specs,
        out_shape=out_shapes,
        scratch_shapes=[
            pltpu.VMEM((3, SUBLANES, A_WIDTH), F32),
            pltpu.VMEM((3, SUBLANES, A_WIDTH), F32),
            pltpu.VMEM((tm, COLS_A + COLS_B), F32),
            pltpu.VMEM((B_WIDTH + B_KV_WIDTH, tm), F32),
        ],
        compiler_params=pltpu.CompilerParams(
            dimension_semantics=("arbitrary", "arbitrary"),
            vmem_limit_bytes=VMEM_LIMIT_BYTES),
        name="inproj",
    )(x, w_in_all, wb, wt, cw, alog_vec, dtb_vec)


def _split3(x):
    hi = x.astype(BF16)
    r1 = x - hi.astype(F32)
    mid = r1.astype(BF16)
    lo = (r1 - mid.astype(F32)).astype(BF16)
    return hi, mid, lo


def _delta_kernel(q_ref, k_ref, v_ref, bg_ref, nw_ref, o_ref, s_ref):
    tb = q_ref.shape[1]
    n_chunks = tb // CHUNK
    t = pl.program_id(1)

    @pl.when(t == 0)
    def _():
        s_ref[...] = jnp.zeros_like(s_ref)

    bg = bg_ref[0]
    r_i = lax.broadcasted_iota(jnp.int32, (CHUNK, CHUNK), 0)
    c_i = lax.broadcasted_iota(jnp.int32, (CHUNK, CHUNK), 1)
    tril = jnp.where(c_i <= r_i, 1.0, 0.0).astype(BF16)
    bg_cat = jnp.concatenate([bg[c * CHUNK:(c + 1) * CHUNK] for c in range(n_chunks)], axis=1)
    hi, mid, lo = _split3(bg_cat)
    gcum_cat = _dot(tril, hi) + _dot(tril, mid) + _dot(tril, lo)

    row = lax.broadcasted_iota(jnp.int32, (CHUNK, LANES), 0)
    lane = lax.broadcasted_iota(jnp.int32, (CHUNK, LANES), 1)
    col = jnp.where(lane >= CHUNK, lane - CHUNK, lane)
    causal = col <= row
    strict = col < row
    lo_half = lane < CHUNK
    lo_row = lo_half[0:1]
    eye2 = jnp.where(col == row, 1.0, 0.0)
    nw = nw_ref[...]
    half = LANES // 2

    csl = lambda c: slice(c * CHUNK, (c + 1) * CHUNK)
    hsl = lambda hd: slice(hd * A_HEAD_DIM, (hd + 1) * A_HEAD_DIM)
    gcums = [gcum_cat[:, c * LANES:(c + 1) * LANES] for c in range(n_chunks)]
    g2ts = [jnp.concatenate([g, g], axis=0).T for g in gcums]
    gidx = lambda hd: slice(A_HEADS + hd, A_HEADS + hd + 1)

    st = {}
    chains = {}

    def stage_products(c, m):
        ha, hb = 2 * m, 2 * m + 1
        gcum = gcums[c]
        qs = [q_ref[0, csl(c), hsl(h)] for h in (ha, hb)]
        ks = [k_ref[0, csl(c), hsl(h)] for h in (ha, hb)]
        gcols = [gcum[:, gidx(h)] for h in (ha, hb)]
        bcols = [bg[csl(c), h:h + 1] for h in (ha, hb)]
        glasts = [gcum[CHUNK - 1:CHUNK, gidx(h)] for h in (ha, hb)]
        gcol_p = jnp.where(lo_half, gcols[0], gcols[1])
        grow_p = jnp.where(lo_row, g2ts[c][gidx(ha), :], g2ts[c][gidx(hb), :])
        bcol_p = jnp.where(lo_half, bcols[0], bcols[1])
        decay = jnp.where(causal, jnp.exp(jnp.minimum(gcol_p - grow_p, 0.0)), 0.0)
        kt = jnp.concatenate(ks, axis=0).astype(F32).T
        prod = _dot(jnp.concatenate([qs[0], ks[0], qs[1], ks[1]], axis=0), kt.astype(BF16))
        qk = jnp.where(lo_half, prod[0:CHUNK], prod[2 * CHUNK:3 * CHUNK]) * decay
        kk = jnp.where(lo_half, prod[CHUNK:2 * CHUNK], prod[3 * CHUNK:])
        glast_row = jnp.where(lo_row, glasts[0], glasts[1])
        st[(c, m)] = dict(qs=qs, ks=ks, gcols=gcols, bcols=bcols, glasts=glasts, qk=qk,
                          kdt=kt * jnp.exp(glast_row - grow_p),
                          t=eye2, p=jnp.where(strict, -(kk * decay * bcol_p), 0.0))

    n_levels = 6

    def stage_double(pr, lvl):
        d = st[pr]
        pb = d["p"].astype(BF16)
        tb16 = d["t"].astype(BF16)
        zero = jnp.zeros_like(pb)
        if lvl < n_levels - 1:
            rhs = jnp.concatenate(
                [jnp.concatenate([jnp.where(lo_half, tb16, zero), jnp.where(lo_half, pb, zero)], axis=1),
                 jnp.concatenate([jnp.where(lo_half, zero, tb16), jnp.where(lo_half, zero, pb)], axis=1)],
                axis=0)
            r = _dot(pb, rhs)
            d["t"] = d["t"] + r[:, :LANES]
            d["p"] = r[:, LANES:]
        else:
            rhs = jnp.concatenate([jnp.where(lo_half, tb16, zero), jnp.where(lo_half, zero, tb16)], axis=0)
            d["t"] = d["t"] + _dot(pb, rhs)

    def stage_uw(c, m):
        d = st[(c, m)]
        t_sw = pltpu.roll(d["t"], half, axis=1)
        kdt_sw = pltpu.roll(d["kdt"], half, axis=1)
        qk_sw = pltpu.roll(d["qk"], half, axis=1)
        for i, (t_x, kdt_x, qk_x) in enumerate(((d["t"], d["kdt"], d["qk"]), (t_sw, kdt_sw, qk_sw))):
            hd = 2 * m + i
            kf = d["ks"][i].astype(F32)
            vf = v_ref[0, csl(c), hsl(hd)].astype(F32)
            bcol, gcol, glast = d["bcols"][i], d["gcols"][i], d["glasts"][i]
            eg = jnp.exp(gcol)
            x_cat = jnp.concatenate([vf * bcol, kf * (bcol * eg)], axis=1).astype(BF16)
            uw = _dot(t_x[:, :half].astype(BF16), x_cat)
            lhs = jnp.concatenate([kdt_x[:, :half], qk_x[:, :half]], axis=0).astype(BF16)
            chains[(c, hd)] = dict(uw=uw.astype(BF16), lhs=lhs, eg=eg, glast=glast, q=d["qs"][i])

    def stage_cm(key):
        d = chains[key]
        r = _dot(d["lhs"], d["uw"])
        d["cm"] = r[:A_HEAD_DIM, :A_HEAD_DIM]
        d["mm"] = r[:A_HEAD_DIM, A_HEAD_DIM:]
        d["oi"] = r[A_HEAD_DIM:, :A_HEAD_DIM]
        d["qp"] = d["q"].astype(F32) * d["eg"] - r[A_HEAD_DIM:, A_HEAD_DIM:]

    def stage_seq(c):
        for hd in range(A_HEADS):
            d = chains[(c, hd)]
            s_old = s_ref[hd]
            lhs_seq = jnp.concatenate([d["mm"], d["qp"]], axis=0).astype(BF16)
            r_seq = _dot(lhs_seq, s_old.astype(BF16))
            s_ref[hd] = jnp.exp(d["glast"]) * s_old + d["cm"] - r_seq[:A_HEAD_DIM]
            o = r_seq[A_HEAD_DIM:] + d["oi"]
            ms = jnp.mean(o * o, axis=-1, keepdims=True)
            o_ref[0, csl(c), hsl(hd)] = (o * lax.rsqrt(ms + RMS_EPS) * nw).astype(o_ref.dtype)

    group = n_chunks // N_GROUPS
    group_slots = []
    for g0 in range(0, n_chunks, group):
        gp = [(c, m) for c in range(g0, g0 + group) for m in range(A_HEADS // 2)]
        slots = [lambda gp=gp: [stage_products(c, m) for (c, m) in gp]]
        for lvl in range(n_levels):
            slots.append(lambda gp=gp, lvl=lvl: [stage_double(pr, lvl) for pr in gp])
        slots.append(lambda gp=gp: [stage_uw(c, m) for (c, m) in gp])
        slots.append(lambda gp=gp: [stage_cm((c, 2 * m + i)) for (c, m) in gp for i in range(2)])
        for c in range(g0, g0 + group):
            slots.append(lambda c=c: stage_seq(c))
        group_slots.append(slots)
    n_slots = len(group_slots[0])
    for tick in range(n_slots + SKEW * (N_GROUPS - 1)):
        for g, slots in enumerate(group_slots):
            k = tick - g * SKEW
            if 0 <= k < n_slots:
                slots[k]()


def _delta(qa, ka, va, bg, nw):
    bsz, t_len, _ = qa.shape
    tb = TB_DELTA
    grid = (bsz, t_len // tb)
    tok = lambda width: pl.BlockSpec((1, tb, width), lambda b, t: (b, t, 0))
    return pl.pallas_call(
        _delta_kernel,
        grid=grid,
        in_specs=[tok(A_WIDTH), tok(A_WIDTH), tok(A_WIDTH), tok(LANES),
                  pl.BlockSpec(nw.shape, lambda b, t: (0, 0))],
        out_specs=tok(A_WIDTH),
        out_shape=jax.ShapeDtypeStruct((bsz, t_len, A_WIDTH), BF16),
        scratch_shapes=[pltpu.VMEM((A_HEADS, A_HEAD_DIM, A_HEAD_DIM), F32)],
        compiler_params=pltpu.CompilerParams(
            dimension_semantics=("arbitrary", "arbitrary"),
            vmem_limit_bytes=VMEM_LIMIT_BYTES),
        name="delta",
    )(qa, ka, va, bg, nw)


def _swa_kernel(qt_ref, kp_ref, kc_ref, vtp_ref, vtc_ref, slope_ref, sink_ref, o_ref):
    n = pl.program_id(1)
    nqb = qt_ref.shape[2] // BLOCK
    band = 2 * BLOCK
    hq = B_Q_HEADS * BLOCK
    k_all = jnp.concatenate([kp_ref[0], kc_ref[0]], axis=0)
    vt_all = jnp.concatenate([vtp_ref[0], vtc_ref[0]], axis=1)

    s_idx = lax.broadcasted_iota(jnp.int32, (band, BLOCK), 0)
    q_idx = lax.broadcasted_iota(jnp.int32, (band, BLOCK), 1)
    dist = q_idx + BLOCK - s_idx
    valid = (dist >= 0) & (dist < WINDOW)
    dist_f = dist.astype(F32)
    hconst = lambda ref, h, shape: jnp.broadcast_to(ref[h:h + 1, 0:1], shape)
    bias = [jnp.where(valid, -(hconst(slope_ref, h, (band, BLOCK)) * LOG2E) * dist_f, -jnp.inf)
            for h in range(B_Q_HEADS)]
    sink = [hconst(sink_ref, h, (1, BLOCK)) * LOG2E for h in range(B_Q_HEADS)]
    no_prev = s_idx < BLOCK
    zq = jnp.zeros((B_HEAD_DIM, B_GROUP * BLOCK), BF16)

    scores = []
    for i in range(nqb):
        qs = slice(i * BLOCK, (i + 1) * BLOCK)
        heads = [qt_ref[0, h * B_HEAD_DIM:(h + 1) * B_HEAD_DIM, qs] for h in range(B_Q_HEADS)]
        q_rhs = jnp.concatenate(
            [jnp.concatenate(heads[:B_GROUP] + [zq], axis=1),
             jnp.concatenate([zq] + heads[B_GROUP:], axis=1)], axis=0)
        scores.append(_dot(k_all[i * BLOCK:i * BLOCK + band], q_rhs))

    probs, invs = [], []
    for i in range(nqb):
        p_heads, inv_heads = [], []
        for h in range(B_Q_HEADS):
            s = scores[i][:, h * BLOCK:(h + 1) * BLOCK] + bias[h]
            if i == 0:
                s = jnp.where(no_prev & (n == 0), -jnp.inf, s)
            m = jnp.maximum(jnp.max(s, axis=0, keepdims=True), sink[h])
            p = jnp.exp2(s - m)
            den = jnp.sum(p, axis=0, keepdims=True) + jnp.exp2(sink[h] - m)
            p_heads.append(p.astype(BF16))
            inv_heads.append(1.0 / den)
        probs.append(jnp.concatenate(p_heads, axis=1))
        invs.append(inv_heads)

    outs = [_dot(vt_all[:, i * BLOCK:i * BLOCK + band], probs[i]) for i in range(nqb)]

    for i in range(nqb):
        qs = slice(i * BLOCK, (i + 1) * BLOCK)
        for j in range(B_Q_HEADS // 2):
            r0 = (2 * j // B_GROUP) * B_HEAD_DIM
            pair_t = jnp.concatenate(
                [outs[i][r0:r0 + B_HEAD_DIM, (2 * j + c) * BLOCK:(2 * j + c + 1) * BLOCK] * invs[i][2 * j + c]
                 for c in range(2)], axis=0)
            o_ref[0, qs, j * LANES:(j + 1) * LANES] = pair_t.T.astype(o_ref.dtype)


def _swa(qbt, kb, vbt, slope_rows, sink_rows):
    bsz, _, t_len = qbt.shape
    nqb = QB_SWA
    tq = nqb * BLOCK
    grid = (bsz, t_len // tq)
    prev_blk = lambda n: jnp.maximum(n * nqb - 1, 0)
    full = lambda a: pl.BlockSpec(a.shape, lambda b, n: (0,) * a.ndim)
    return pl.pallas_call(
        _swa_kernel,
        grid=grid,
        in_specs=[pl.BlockSpec((1, B_WIDTH, tq), lambda b, n: (b, 0, n)),
                  pl.BlockSpec((1, BLOCK, B_KV_WIDTH), lambda b, n: (b, prev_blk(n), 0)),
                  pl.BlockSpec((1, tq, B_KV_WIDTH), lambda b, n: (b, n, 0)),
                  pl.BlockSpec((1, B_KV_WIDTH, BLOCK), lambda b, n: (b, 0, prev_blk(n))),
                  pl.BlockSpec((1, B_KV_WIDTH, tq), lambda b, n: (b, 0, n)),
                  full(slope_rows), full(sink_rows)],
        out_specs=pl.BlockSpec((1, tq, B_WIDTH), lambda b, n: (b, n, 0)),
        out_shape=jax.ShapeDtypeStruct((bsz, t_len, B_WIDTH), BF16),
        compiler_params=pltpu.CompilerParams(
            dimension_semantics=("arbitrary", "arbitrary"),
            vmem_limit_bytes=VMEM_LIMIT_BYTES),
        name="swa",
    )(qbt, kb, kb, vbt, vbt, slope_rows, sink_rows)


def _outproj_kernel(alpha, x_ref, ya_ref, za_ref, yb_ref, zb_ref, w_ref, g_ref, b_ref, o_ref):
    ya = (ya_ref[0].astype(F32) * _silu(za_ref[0].astype(F32))).astype(BF16)
    yb = (yb_ref[0].astype(F32) * _silu(zb_ref[0].astype(F32))).astype(BF16)
    y = _dot(ya, w_ref[:A_WIDTH, :]) + _dot(yb, w_ref[A_WIDTH:, :])
    r = alpha * x_ref[0] + y
    mu = jnp.mean(r, axis=-1, keepdims=True)
    rc = r - mu
    var = jnp.mean(rc * rc, axis=-1, keepdims=True)
    o_ref[0] = rc * lax.rsqrt(var + LN_EPS) * g_ref[...] + b_ref[...]


def _outproj(x, ya, za, yb, zb, w_out_all, layer, ln_g, ln_b, alpha):
    bsz, t_len, d = x.shape
    tm = TM_OUT
    grid = (bsz, t_len // tm)
    tok = lambda width: pl.BlockSpec((1, tm, width), lambda b, t: (b, t, 0))
    full = lambda a: pl.BlockSpec(a.shape, lambda b, t: (0,) * a.ndim)
    w_spec = pl.BlockSpec((None,) + w_out_all.shape[1:], lambda b, t: (layer, 0, 0))
    return pl.pallas_call(
        functools.partial(_outproj_kernel, alpha),
        grid=grid,
        in_specs=[tok(d), tok(A_WIDTH), tok(A_WIDTH), tok(B_WIDTH), tok(B_WIDTH), w_spec,
                  full(ln_g), full(ln_b)],
        out_specs=tok(d),
        out_shape=jax.ShapeDtypeStruct((bsz, t_len, d), F32),
        compiler_params=pltpu.CompilerParams(
            dimension_semantics=("arbitrary", "arbitrary"),
            vmem_limit_bytes=VMEM_LIMIT_BYTES),
        name="outproj",
    )(x, ya, za, yb, zb, w_out_all, ln_g, ln_b)


def _transpose_kernel(w_ref, o_ref):
    o_ref[0] = w_ref[0].astype(F32).T.astype(o_ref.dtype)


def _transpose_weights(w):
    depth, d, n = w.shape
    return pl.pallas_call(
        _transpose_kernel,
        grid=(depth,),
        in_specs=[pl.BlockSpec((1, d, n), lambda l: (l, 0, 0))],
        out_specs=pl.BlockSpec((1, n, d), lambda l: (l, 0, 0)),
        out_shape=jax.ShapeDtypeStruct((depth, n, d), BF16),
        compiler_params=pltpu.CompilerParams(
            dimension_semantics=("arbitrary",), vmem_limit_bytes=VMEM_LIMIT_BYTES),
        name="wtranspose",
    )(w)


def _lane_rows(vals):
    return jnp.broadcast_to(vals.astype(F32)[:, None], (vals.shape[0], LANES))


def kernel(x, w_in, conv_w, a_log, dt_bias, norm_w, sinks, w_out, ln_g, ln_b):
    depth, d, _ = w_in.shape
    alpha = (2 * depth) ** 0.25
    w_in_bf = w_in.astype(BF16)
    w_out_bf = w_out.astype(BF16)
    o_small = COLS_A
    o_qb = o_small + 2 * A_HEADS
    o_kb = o_qb + B_WIDTH
    o_vb = o_kb + B_KV_WIDTH
    o_zb = o_vb + B_KV_WIDTH
    wb_all = jnp.concatenate(
        [w_in_bf[:, :, o_kb:o_vb], w_in_bf[:, :, o_zb:], w_in_bf[:, :, o_small:o_qb],
         jnp.zeros((depth, d, LANES - 2 * A_HEADS), BF16)], axis=2)
    wt_all = _transpose_weights(
        jnp.concatenate([w_in_bf[:, :, o_qb:o_kb], w_in_bf[:, :, o_vb:o_zb]], axis=2))
    pad = jnp.zeros((LANES - 2 * A_HEADS,), F32)
    zeros_h = jnp.zeros((A_HEADS,), F32)
    slope_rows = _lane_rows(
        jnp.asarray([2.0 ** (-8.0 * (h + 1) / B_Q_HEADS) for h in range(B_Q_HEADS)], F32))
    for l in range(depth):
        alog_vec = jnp.concatenate([zeros_h, a_log[l].astype(F32), pad])[None, :]
        dtb_vec = jnp.concatenate([zeros_h, dt_bias[l].astype(F32), pad])[None, :]
        qa, ka, va, za, bg, qbt, kb, vbt, zb = _inproj(
            x, w_in_bf, l, wb_all[l], wt_all[l], conv_w[l].astype(F32), alog_vec, dtb_vec)
        ya = _delta(qa, ka, va, bg, norm_w[l].astype(F32)[None, :])
        yb = _swa(qbt, kb, vbt, slope_rows, _lane_rows(sinks[l]))
        x = _outproj(x, ya, za, yb, zb, w_out_bf, l, ln_g[l].astype(F32)[None, :],
                     ln_b[l].astype(F32)[None, :], alpha)
    return x
```

```python
import functools

import jax
import jax.numpy as jnp
from jax import lax
from jax.experimental import pallas as pl
from jax.experimental.pallas import tpu as pltpu

F32 = jnp.float32
BF16 = jnp.bfloat16

LANES = 128
SUBLANES = 8
VMEM_LIMIT_BYTES = 48 * 1024 * 1024

A_HEADS = 4
A_HEAD_DIM = 128
A_WIDTH = A_HEADS * A_HEAD_DIM
CONV_K = 4
CHUNK = 64
B_Q_HEADS = 8
B_KV_HEADS = 2
B_HEAD_DIM = 64
B_GROUP = B_Q_HEADS // B_KV_HEADS
B_WIDTH = B_Q_HEADS * B_HEAD_DIM
B_KV_WIDTH = B_KV_HEADS * B_HEAD_DIM
WINDOW = 128
BLOCK = 128
LN_EPS = 1e-5
RMS_EPS = 1e-6
L2_EPS = 1e-6
LOG2E = 1.4426950408889634

COLS_A = 4 * A_WIDTH
COL_KB = 0
COL_ZB = COL_KB + B_KV_WIDTH
COL_SMALL = COL_ZB + B_WIDTH
COLS_B = COL_SMALL + LANES

TM_PROJ = 512
PROJ_COLS = 256
TM_OUT = 1024
TB_DELTA = 2048
N_GROUPS = 32
SKEW = 1
QB_SWA = 8


def _silu(x):
    return x * (1.0 / (1.0 + jnp.exp(-x)))


def _sigmoid(x):
    return 1.0 / (1.0 + jnp.exp(-x))


def _softplus(x):
    return jnp.maximum(x, 0.0) + jnp.log(1.0 + jnp.exp(-jnp.abs(x)))


def _dot(a, b):
    return jnp.dot(a, b, preferred_element_type=F32)


def _dot_nt(a, b):
    return lax.dot_general(a, b, (((1,), (1,)), ((), ())), preferred_element_type=F32)


def _inproj_kernel(x_ref, wa_ref, wb_ref, wt_ref, cw_ref, alog_ref, dtb_ref,
                   qa_ref, ka_ref, va_ref, za_ref, bg_ref, qbt_ref, kb_ref, vbt_ref, zb_ref,
                   prev_ref, prevr_ref, scr_ref, scrt_ref):
    tm = x_ref.shape[1]
    t = pl.program_id(1)

    @pl.when(t == 0)
    def _():
        prev_ref[...] = jnp.zeros_like(prev_ref)
        prevr_ref[...] = jnp.zeros_like(prevr_ref)

    xb = x_ref[0].astype(BF16)

    for c0 in range(0, COLS_A, PROJ_COLS):
        scr_ref[:, c0:c0 + PROJ_COLS] = _dot(xb, wa_ref[:, c0:c0 + PROJ_COLS])
    scrt_ref[...] = _dot_nt(wt_ref[...], xb)
    for c0 in range(0, COLS_B, PROJ_COLS):
        scr_ref[:, COLS_A + c0:COLS_A + c0 + PROJ_COLS] = _dot(xb, wb_ref[:, c0:c0 + PROJ_COLS])

    row8 = lax.broadcasted_iota(jnp.int32, (SUBLANES, A_HEAD_DIM), 0)

    def shift_rows(x, carry, s):
        y = pltpu.roll(x, s, axis=0)
        top = jnp.where(row8 < s, pltpu.roll(carry, s, axis=0), y[0:SUBLANES])
        return jnp.concatenate([top, y[SUBLANES:]], axis=0)

    def conv_group(gi, out_ref):
        c0 = gi * A_WIDTH
        scale = A_HEAD_DIM ** -0.5 if gi == 0 else 1.0
        for hd in range(A_HEADS):
            sl = slice(hd * A_HEAD_DIM, (hd + 1) * A_HEAD_DIM)
            wsl = slice(c0 + hd * A_HEAD_DIM, c0 + (hd + 1) * A_HEAD_DIM)
            w0, w1, w2, w3 = [cw_ref[j:j + 1, wsl] for j in range(CONV_K)]
            hc = scr_ref[:, wsl]
            h1 = shift_rows(hc, prev_ref[gi, :, sl], 1)
            r = w1 * hc + w0 * h1
            acc = w3 * hc + w2 * h1 + shift_rows(r, prevr_ref[gi, :, sl], 2)
            prev_ref[gi, :, sl] = hc[tm - SUBLANES:tm]
            prevr_ref[gi, :, sl] = r[tm - SUBLANES:tm]
            a = _silu(acc)
            if gi < 2:
                ss = jnp.sum(a * a, axis=-1, keepdims=True)
                a = a * (lax.rsqrt(ss + L2_EPS) * scale)
            out_ref[0, :, sl] = a.astype(out_ref.dtype)

    conv_group(0, qa_ref)
    conv_group(1, ka_ref)
    conv_group(2, va_ref)
    za_ref[0] = scr_ref[:, 3 * A_WIDTH:4 * A_WIDTH].astype(za_ref.dtype)
    qbt_ref[0] = (scrt_ref[:B_WIDTH, :] * (B_HEAD_DIM ** -0.5 * LOG2E)).astype(qbt_ref.dtype)
    vbt_ref[0] = scrt_ref[B_WIDTH:, :].astype(vbt_ref.dtype)
    kb_ref[0] = scr_ref[:, COLS_A + COL_KB:COLS_A + COL_KB + B_KV_WIDTH].astype(kb_ref.dtype)
    zb_ref[0] = scr_ref[:, COLS_A + COL_ZB:COLS_A + COL_ZB + B_WIDTH].astype(zb_ref.dtype)
    hs = scr_ref[:, COLS_A + COL_SMALL:COLS_A + COL_SMALL + LANES]

    lane = lax.broadcasted_iota(jnp.int32, hs.shape, 1)
    beta = _sigmoid(hs)
    g = -jnp.exp(alog_ref[...]) * _softplus(hs + dtb_ref[...])
    bg_ref[0] = jnp.where(lane < A_HEADS, beta, g)


def _inproj(x, w_in_all, layer, wb, wt, cw, alog_vec, dtb_vec):
    bsz, t_len, d = x.shape
    tm = TM_PROJ
    grid = (bsz, t_len // tm)
    tok = lambda width: pl.BlockSpec((1, tm, width), lambda b, t: (b, t, 0))
    tok_t = lambda rows: pl.BlockSpec((1, rows, tm), lambda b, t: (b, 0, t))
    full = lambda a: pl.BlockSpec(a.shape, lambda b, t: (0,) * a.ndim)
    wa_spec = pl.BlockSpec((None, d, COLS_A), lambda b, t: (layer, 0, 0))
    out_shapes = [
        jax.ShapeDtypeStruct((bsz, t_len, A_WIDTH), BF16),
        jax.ShapeDtypeStruct((bsz, t_len, A_WIDTH), BF16),
        jax.ShapeDtypeStruct((bsz, t_len, A_WIDTH), BF16),
        jax.ShapeDtypeStruct((bsz, t_len, A_WIDTH), BF16),
        jax.ShapeDtypeStruct((bsz, t_len, LANES), F32),
        jax.ShapeDtypeStruct((bsz, B_WIDTH, t_len), BF16),
        jax.ShapeDtypeStruct((bsz, t_len, B_KV_WIDTH), BF16),
        jax.ShapeDtypeStruct((bsz, B_KV_WIDTH, t_len), BF16),
        jax.ShapeDtypeStruct((bsz, t_len, B_WIDTH), BF16),
    ]
    out_specs = [tok(A_WIDTH), tok(A_WIDTH), tok(A_WIDTH), tok(A_WIDTH), tok(LANES),
                 tok_t(B_WIDTH), tok(B_KV_WIDTH), tok_t(B_KV_WIDTH), tok(B_WIDTH)]
    return pl.pallas_call(
        _inproj_kernel,
        grid=grid,
        in_specs=[tok(d), wa_spec, full(wb), full(wt), full(cw), full(alog_vec), full(dtb_vec)],
        out_specs=out_specs,
        out_shape=out_shapes,
        scratch_shapes=[
            pltpu.VMEM((3, SUBLANES, A_WIDTH), F32),
            pltpu.VMEM((3, SUBLANES, A_WIDTH), F32),
            pltpu.VMEM((tm, COLS_A + COLS_B), F32),
            pltpu.VMEM((B_WIDTH + B_KV_WIDTH, tm), F32),
        ],
        compiler_params=pltpu.CompilerParams(
            dimension_semantics=("arbitrary", "arbitrary"),
            vmem_limit_bytes=VMEM_LIMIT_BYTES),
        name="inproj",
    )(x, w_in_all, wb, wt, cw, alog_vec, dtb_vec)


def _split3(x):
    hi = x.astype(BF16)
    r1 = x - hi.astype(F32)
    mid = r1.astype(BF16)
    lo = (r1 - mid.astype(F32)).astype(BF16)
    return hi, mid, lo


def _delta_kernel(q_ref, k_ref, v_ref, bg_ref, z_ref, nw_ref, o_ref, s_ref):
    tb = q_ref.shape[1]
    n_chunks = tb // CHUNK
    t = pl.program_id(1)

    @pl.when(t == 0)
    def _():
        s_ref[...] = jnp.zeros_like(s_ref)

    bg = bg_ref[0]
    r_i = lax.broadcasted_iota(jnp.int32, (CHUNK, CHUNK), 0)
    c_i = lax.broadcasted_iota(jnp.int32, (CHUNK, CHUNK), 1)
    tril = jnp.where(c_i <= r_i, 1.0, 0.0).astype(BF16)
    bg_cat = jnp.concatenate([bg[c * CHUNK:(c + 1) * CHUNK] for c in range(n_chunks)], axis=1)
    hi, mid, lo = _split3(bg_cat)
    gcum_cat = _dot(tril, hi) + _dot(tril, mid) + _dot(tril, lo)

    row = lax.broadcasted_iota(jnp.int32, (CHUNK, LANES), 0)
    lane = lax.broadcasted_iota(jnp.int32, (CHUNK, LANES), 1)
    col = jnp.where(lane >= CHUNK, lane - CHUNK, lane)
    causal = col <= row
    strict = col < row
    lo_half = lane < CHUNK
    lo_row = lo_half[0:1]
    eye2 = jnp.where(col == row, 1.0, 0.0)
    nw = nw_ref[...]
    half = LANES // 2

    csl = lambda c: slice(c * CHUNK, (c + 1) * CHUNK)
    hsl = lambda hd: slice(hd * A_HEAD_DIM, (hd + 1) * A_HEAD_DIM)
    gcums = [gcum_cat[:, c * LANES:(c + 1) * LANES] for c in range(n_chunks)]
    g2ts = [jnp.concatenate([g, g], axis=0).T for g in gcums]
    gidx = lambda hd: slice(A_HEADS + hd, A_HEADS + hd + 1)

    st = {}
    chains = {}

    def stage_products(c, m):
        ha, hb = 2 * m, 2 * m + 1
        gcum = gcums[c]
        qs = [q_ref[0, csl(c), hsl(h)] for h in (ha, hb)]
        ks = [k_ref[0, csl(c), hsl(h)] for h in (ha, hb)]
        gcols = [gcum[:, gidx(h)] for h in (ha, hb)]
        bcols = [bg[csl(c), h:h + 1] for h in (ha, hb)]
        glasts = [gcum[CHUNK - 1:CHUNK, gidx(h)] for h in (ha, hb)]
        gcol_p = jnp.where(lo_half, gcols[0], gcols[1])
        grow_p = jnp.where(lo_row, g2ts[c][gidx(ha), :], g2ts[c][gidx(hb), :])
        bcol_p = jnp.where(lo_half, bcols[0], bcols[1])
        decay = jnp.where(causal, jnp.exp(jnp.minimum(gcol_p - grow_p, 0.0)), 0.0)
        kt = jnp.concatenate(ks, axis=0).astype(F32).T
        prod = _dot(jnp.concatenate([qs[0], ks[0], qs[1], ks[1]], axis=0), kt.astype(BF16))
        qk = jnp.where(lo_half, prod[0:CHUNK], prod[2 * CHUNK:3 * CHUNK]) * decay
        kk = jnp.where(lo_half, prod[CHUNK:2 * CHUNK], prod[3 * CHUNK:])
        glast_row = jnp.where(lo_row, glasts[0], glasts[1])
        st[(c, m)] = dict(qs=qs, ks=ks, gcols=gcols, bcols=bcols, glasts=glasts, qk=qk,
                          kdt=kt * jnp.exp(glast_row - grow_p),
                          t=eye2, p=jnp.where(strict, -(kk * decay * bcol_p), 0.0))

    n_levels = 6

    def stage_double(pr, lvl):
        d = st[pr]
        pb = d["p"].astype(BF16)
        tb16 = d["t"].astype(BF16)
        zero = jnp.zeros_like(pb)
        if lvl < n_levels - 1:
            rhs = jnp.concatenate(
                [jnp.concatenate([jnp.where(lo_half, tb16, zero), jnp.where(lo_half, pb, zero)], axis=1),
                 jnp.concatenate([jnp.where(lo_half, zero, tb16), jnp.where(lo_half, zero, pb)], axis=1)],
                axis=0)
            r = _dot(pb, rhs)
            d["t"] = d["t"] + r[:, :LANES]
            d["p"] = r[:, LANES:]
        else:
            rhs = jnp.concatenate([jnp.where(lo_half, tb16, zero), jnp.where(lo_half, zero, tb16)], axis=0)
            d["t"] = d["t"] + _dot(pb, rhs)

    def stage_uw(c, m):
        d = st[(c, m)]
        t_sw = pltpu.roll(d["t"], half, axis=1)
        kdt_sw = pltpu.roll(d["kdt"], half, axis=1)
        qk_sw = pltpu.roll(d["qk"], half, axis=1)
        for i, (t_x, kdt_x, qk_x) in enumerate(((d["t"], d["kdt"], d["qk"]), (t_sw, kdt_sw, qk_sw))):
            hd = 2 * m + i
            kf = d["ks"][i].astype(F32)
            vf = v_ref[0, csl(c), hsl(hd)].astype(F32)
            bcol, gcol, glast = d["bcols"][i], d["gcols"][i], d["glasts"][i]
            eg = jnp.exp(gcol)
            x_cat = jnp.concatenate([vf * bcol, kf * (bcol * eg)], axis=1).astype(BF16)
            uw = _dot(t_x[:, :half].astype(BF16), x_cat)
            lhs = jnp.concatenate([kdt_x[:, :half], qk_x[:, :half]], axis=0).astype(BF16)
            chains[(c, hd)] = dict(uw=uw.astype(BF16), lhs=lhs, eg=eg, glast=glast, q=d["qs"][i])

    def stage_cm(key):
        d = chains[key]
        r = _dot(d["lhs"], d["uw"])
        d["cm"] = r[:A_HEAD_DIM, :A_HEAD_DIM]
        d["mm"] = r[:A_HEAD_DIM, A_HEAD_DIM:]
        d["oi"] = r[A_HEAD_DIM:, :A_HEAD_DIM]
        d["qp"] = d["q"].astype(F32) * d["eg"] - r[A_HEAD_DIM:, A_HEAD_DIM:]

    def stage_seq(c):
        for hd in range(A_HEADS):
            d = chains[(c, hd)]
            s_old = s_ref[hd]
            lhs_seq = jnp.concatenate([d["mm"], d["qp"]], axis=0).astype(BF16)
            r_seq = _dot(lhs_seq, s_old.astype(BF16))
            s_ref[hd] = jnp.exp(d["glast"]) * s_old + d["cm"] - r_seq[:A_HEAD_DIM]
            o = r_seq[A_HEAD_DIM:] + d["oi"]
            ms = jnp.mean(o * o, axis=-1, keepdims=True)
            gate = _silu(z_ref[0, csl(c), hsl(hd)].astype(F32))
            o_ref[0, csl(c), hsl(hd)] = (o * lax.rsqrt(ms + RMS_EPS) * nw * gate).astype(o_ref.dtype)

    group = n_chunks // N_GROUPS
    group_slots = []
    for g0 in range(0, n_chunks, group):
        gp = [(c, m) for c in range(g0, g0 + group) for m in range(A_HEADS // 2)]
        slots = [lambda gp=gp: [stage_products(c, m) for (c, m) in gp]]
        for lvl in range(n_levels):
            slots.append(lambda gp=gp, lvl=lvl: [stage_double(pr, lvl) for pr in gp])
        slots.append(lambda gp=gp: [stage_uw(c, m) for (c, m) in gp])
        slots.append(lambda gp=gp: [stage_cm((c, 2 * m + i)) for (c, m) in gp for i in range(2)])
        for c in range(g0, g0 + group):
            slots.append(lambda c=c: stage_seq(c))
        group_slots.append(slots)
    n_slots = len(group_slots[0])
    for tick in range(n_slots + SKEW * (N_GROUPS - 1)):
        for g, slots in enumerate(group_slots):
            k = tick - g * SKEW
            if 0 <= k < n_slots:
                slots[k]()


def _delta(qa, ka, va, bg, za, nw):
    bsz, t_len, _ = qa.shape
    tb = TB_DELTA
    grid = (bsz, t_len // tb)
    tok = lambda width: pl.BlockSpec((1, tb, width), lambda b, t: (b, t, 0))
    return pl.pallas_call(
        _delta_kernel,
        grid=grid,
        in_specs=[tok(A_WIDTH), tok(A_WIDTH), tok(A_WIDTH), tok(LANES), tok(A_WIDTH),
                  pl.BlockSpec(nw.shape, lambda b, t: (0, 0))],
        out_specs=tok(A_WIDTH),
        out_shape=jax.ShapeDtypeStruct((bsz, t_len, A_WIDTH), BF16),
        scratch_shapes=[pltpu.VMEM((A_HEADS, A_HEAD_DIM, A_HEAD_DIM), F32)],
        compiler_params=pltpu.CompilerParams(
            dimension_semantics=("arbitrary", "arbitrary"),
            vmem_limit_bytes=VMEM_LIMIT_BYTES),
        name="delta",
    )(qa, ka, va, bg, za, nw)


def _swa_kernel(qt_ref, kp_ref, kc_ref, vtp_ref, vtc_ref, slope_ref, sink_ref, o_ref):
    n = pl.program_id(1)
    nqb = qt_ref.shape[2] // BLOCK
    band = 2 * BLOCK
    hq = B_Q_HEADS * BLOCK
    k_all = jnp.concatenate([kp_ref[0], kc_ref[0]], axis=0)
    vt_all = jnp.concatenate([vtp_ref[0], vtc_ref[0]], axis=1)

    s_idx = lax.broadcasted_iota(jnp.int32, (band, BLOCK), 0)
    q_idx = lax.broadcasted_iota(jnp.int32, (band, BLOCK), 1)
    dist = q_idx + BLOCK - s_idx
    valid = (dist >= 0) & (dist < WINDOW)
    dist_f = dist.astype(F32)
    hconst = lambda ref, h, shape: jnp.broadcast_to(ref[h:h + 1, 0:1], shape)
    bias = [jnp.where(valid, -(hconst(slope_ref, h, (band, BLOCK)) * LOG2E) * dist_f, -jnp.inf)
            for h in range(B_Q_HEADS)]
    sink = [hconst(sink_ref, h, (1, BLOCK)) * LOG2E for h in range(B_Q_HEADS)]
    no_prev = s_idx < BLOCK
    zq = jnp.zeros((B_HEAD_DIM, B_GROUP * BLOCK), BF16)

    scores = []
    for i in range(nqb):
        qs = slice(i * BLOCK, (i + 1) * BLOCK)
        heads = [qt_ref[0, h * B_HEAD_DIM:(h + 1) * B_HEAD_DIM, qs] for h in range(B_Q_HEADS)]
        q_rhs = jnp.concatenate(
            [jnp.concatenate(heads[:B_GROUP] + [zq], axis=1),
             jnp.concatenate([zq] + heads[B_GROUP:], axis=1)], axis=0)
        scores.append(_dot(k_all[i * BLOCK:i * BLOCK + band], q_rhs))

    probs, invs = [], []
    for i in range(nqb):
        p_heads, inv_heads = [], []
        for h in range(B_Q_HEADS):
            s = scores[i][:, h * BLOCK:(h + 1) * BLOCK] + bias[h]
            if i == 0:
                s = jnp.where(no_prev & (n == 0), -jnp.inf, s)
            m = jnp.maximum(jnp.max(s, axis=0, keepdims=True), sink[h])
            p = jnp.exp2(s - m)
            den = jnp.sum(p, axis=0, keepdims=True) + jnp.exp2(sink[h] - m)
            p_heads.append(p.astype(BF16))
            inv_heads.append(1.0 / den)
        probs.append(jnp.concatenate(p_heads, axis=1))
        invs.append(inv_heads)

    outs = [_dot(vt_all[:, i * BLOCK:i * BLOCK + band], probs[i]) for i in range(nqb)]

    for i in range(nqb):
        qs = slice(i * BLOCK, (i + 1) * BLOCK)
        for j in range(B_Q_HEADS // 2):
            r0 = (2 * j // B_GROUP) * B_HEAD_DIM
            pair_t = jnp.concatenate(
                [outs[i][r0:r0 + B_HEAD_DIM, (2 * j + c) * BLOCK:(2 * j + c + 1) * BLOCK] * invs[i][2 * j + c]
                 for c in range(2)], axis=0)
            o_ref[0, qs, j * LANES:(j + 1) * LANES] = pair_t.T.astype(o_ref.dtype)


def _swa(qbt, kb, vbt, slope_rows, sink_rows):
    bsz, _, t_len = qbt.shape
    nqb = QB_SWA
    tq = nqb * BLOCK
    grid = (bsz, t_len // tq)
    prev_blk = lambda n: jnp.maximum(n * nqb - 1, 0)
    full = lambda a: pl.BlockSpec(a.shape, lambda b, n: (0,) * a.ndim)
    return pl.pallas_call(
        _swa_kernel,
        grid=grid,
        in_specs=[pl.BlockSpec((1, B_WIDTH, tq), lambda b, n: (b, 0, n)),
                  pl.BlockSpec((1, BLOCK, B_KV_WIDTH), lambda b, n: (b, prev_blk(n), 0)),
                  pl.BlockSpec((1, tq, B_KV_WIDTH), lambda b, n: (b, n, 0)),
                  pl.BlockSpec((1, B_KV_WIDTH, BLOCK), lambda b, n: (b, 0, prev_blk(n))),
                  pl.BlockSpec((1, B_KV_WIDTH, tq), lambda b, n: (b, 0, n)),
                  full(slope_rows), full(sink_rows)],
        out_specs=pl.BlockSpec((1, tq, B_WIDTH), lambda b, n: (b, n, 0)),
        out_shape=jax.ShapeDtypeStruct((bsz, t_len, B_WIDTH), BF16),
        compiler_params=pltpu.CompilerParams(
            dimension_semantics=("arbitrary", "arbitrary"),
            vmem_limit_bytes=VMEM_LIMIT_BYTES),
        name="swa",
    )(qbt, kb, kb, vbt, vbt, slope_rows, sink_rows)


def _outproj_kernel(alpha, x_ref, ya_ref, yb_ref, zb_ref, w_ref, g_ref, b_ref, o_ref):
    yb = (yb_ref[0].astype(F32) * _silu(zb_ref[0].astype(F32))).astype(BF16)
    y = _dot(ya_ref[0], w_ref[:A_WIDTH, :]) + _dot(yb, w_ref[A_WIDTH:, :])
    r = alpha * x_ref[0] + y
    mu = jnp.mean(r, axis=-1, keepdims=True)
    rc = r - mu
    var = jnp.mean(rc * rc, axis=-1, keepdims=True)
    o_ref[0] = rc * lax.rsqrt(var + LN_EPS) * g_ref[...] + b_ref[...]


def _outproj(x, ya, yb, zb, w_out_all, layer, ln_g, ln_b, alpha):
    bsz, t_len, d = x.shape
    tm = TM_OUT
    grid = (bsz, t_len // tm)
    tok = lambda width: pl.BlockSpec((1, tm, width), lambda b, t: (b, t, 0))
    full = lambda a: pl.BlockSpec(a.shape, lambda b, t: (0,) * a.ndim)
    w_spec = pl.BlockSpec((None,) + w_out_all.shape[1:], lambda b, t: (layer, 0, 0))
    return pl.pallas_call(
        functools.partial(_outproj_kernel, alpha),
        grid=grid,
        in_specs=[tok(d), tok(A_WIDTH), tok(B_WIDTH), tok(B_WIDTH), w_spec,
                  full(ln_g), full(ln_b)],
        out_specs=tok(d),
        out_shape=jax.ShapeDtypeStruct((bsz, t_len, d), F32),
        compiler_params=pltpu.CompilerParams(
            dimension_semantics=("arbitrary", "arbitrary"),
            vmem_limit_bytes=VMEM_LIMIT_BYTES),
        name="outproj",
    )(x, ya, yb, zb, w_out_all, ln_g, ln_b)


def _transpose_kernel(w_ref, o_ref):
    o_ref[0] = w_ref[0].astype(F32).T.astype(o_ref.dtype)


def _transpose_weights(w):
    depth, d, n = w.shape
    return pl.pallas_call(
        _transpose_kernel,
        grid=(depth,),
        in_specs=[pl.BlockSpec((1, d, n), lambda l: (l, 0, 0))],
        out_specs=pl.BlockSpec((1, n, d), lambda l: (l, 0, 0)),
        out_shape=jax.ShapeDtypeStruct((depth, n, d), BF16),
        compiler_params=pltpu.CompilerParams(
            dimension_semantics=("arbitrary",), vmem_limit_bytes=VMEM_LIMIT_BYTES),
        name="wtranspose",
    )(w)


def _lane_rows(vals):
    return jnp.broadcast_to(vals.astype(F32)[:, None], (vals.shape[0], LANES))


def kernel(x, w_in, conv_w, a_log, dt_bias, norm_w, sinks, w_out, ln_g, ln_b):
    depth, d, _ = w_in.shape
    alpha = (2 * depth) ** 0.25
    w_in_bf = w_in.astype(BF16)
    w_out_bf = w_out.astype(BF16)
    o_small = COLS_A
    o_qb = o_small + 2 * A_HEADS
    o_kb = o_qb + B_WIDTH
    o_vb = o_kb + B_KV_WIDTH
    o_zb = o_vb + B_KV_WIDTH
    wb_all = jnp.concatenate(
        [w_in_bf[:, :, o_kb:o_vb], w_in_bf[:, :, o_zb:], w_in_bf[:, :, o_small:o_qb],
         jnp.zeros((depth, d, LANES - 2 * A_HEADS), BF16)], axis=2)
    wt_all = _transpose_weights(
        jnp.concatenate([w_in_bf[:, :, o_qb:o_kb], w_in_bf[:, :, o_vb:o_zb]], axis=2))
    pad = jnp.zeros((LANES - 2 * A_HEADS,), F32)
    zeros_h = jnp.zeros((A_HEADS,), F32)
    slope_rows = _lane_rows(
        jnp.asarray([2.0 ** (-8.0 * (h + 1) / B_Q_HEADS) for h in range(B_Q_HEADS)], F32))
    for l in range(depth):
        alog_vec = jnp.concatenate([zeros_h, a_log[l].astype(F32), pad])[None, :]
        dtb_vec = jnp.concatenate([zeros_h, dt_bias[l].astype(F32), pad])[None, :]
        qa, ka, va, za, bg, qbt, kb, vbt, zb = _inproj(
            x, w_in_bf, l, wb_all[l], wt_all[l], conv_w[l].astype(F32), alog_vec, dtb_vec)
        ya = _delta(qa, ka, va, bg, za, norm_w[l].astype(F32)[None, :])
        yb = _swa(qbt, kb, vbt, slope_rows, _lane_rows(sinks[l]))
        x = _outproj(x, ya, yb, zb, w_out_bf, l, ln_g[l].astype(F32)[None, :],
                     ln_b[l].astype(F32)[None, :], alpha)
    return x
```

```python
import functools

import jax
import jax.numpy as jnp
from jax import lax
from jax.experimental import pallas as pl
from jax.experimental.pallas import tpu as pltpu

F32 = jnp.float32
BF16 = jnp.bfloat16

LANES = 128
SUBLANES = 8
VMEM_LIMIT_BYTES = 48 * 1024 * 1024

A_HEADS = 4
A_HEAD_DIM = 128
A_WIDTH = A_HEADS * A_HEAD_DIM
CONV_K = 4
CHUNK = 64
B_Q_HEADS = 8
B_KV_HEADS = 2
B_HEAD_DIM = 64
B_GROUP = B_Q_HEADS // B_KV_HEADS
B_WIDTH = B_Q_HEADS * B_HEAD_DIM
B_KV_WIDTH = B_KV_HEADS * B_HEAD_DIM
WINDOW = 128
BLOCK = 128
LN_EPS = 1e-5
RMS_EPS = 1e-6
L2_EPS = 1e-6
LOG2E = 1.4426950408889634

COLS_A = 4 * A_WIDTH
COL_KB = 0
COL_ZB = COL_KB + B_KV_WIDTH
COL_SMALL = COL_ZB + B_WIDTH
COLS_B = COL_SMALL + LANES

TM_PROJ = 512
PROJ_COLS = 256
TM_OUT = 1024
TB_DELTA = 2048
N_GROUPS = 32
SKEW = 1
QB_SWA = 8


def _silu(x):
    hx = 0.5 * x
    return hx * jnp.tanh(hx) + hx


def _sigmoid(x):
    return 1.0 / (1.0 + jnp.exp(-x))


def _softplus(x):
    return jnp.maximum(x, 0.0) + jnp.log(1.0 + jnp.exp(-jnp.abs(x)))


def _dot(a, b):
    return jnp.dot(a, b, preferred_element_type=F32)


def _dot_nt(a, b):
    return lax.dot_general(a, b, (((1,), (1,)), ((), ())), preferred_element_type=F32)


def _inproj_kernel(x_ref, wa_ref, wb_ref, wt_ref, cw_ref, alog_ref, dtb_ref,
                   qa_ref, ka_ref, va_ref, za_ref, bg_ref, qbt_ref, kb_ref, vbt_ref, zb_ref,
                   prev_ref, prevr_ref, scr_ref, scrt_ref):
    tm = x_ref.shape[1]
    t = pl.program_id(1)

    @pl.when(t == 0)
    def _():
        prev_ref[...] = jnp.zeros_like(prev_ref)
        prevr_ref[...] = jnp.zeros_like(prevr_ref)

    xb = x_ref[0].astype(BF16)

    for c0 in range(0, COLS_A, PROJ_COLS):
        scr_ref[:, c0:c0 + PROJ_COLS] = _dot(xb, wa_ref[:, c0:c0 + PROJ_COLS])
    scrt_ref[...] = _dot_nt(wt_ref[...], xb)
    for c0 in range(0, COLS_B, PROJ_COLS):
        scr_ref[:, COLS_A + c0:COLS_A + c0 + PROJ_COLS] = _dot(xb, wb_ref[:, c0:c0 + PROJ_COLS])

    row8 = lax.broadcasted_iota(jnp.int32, (SUBLANES, A_HEAD_DIM), 0)

    def shift_rows(x, carry, s):
        y = pltpu.roll(x, s, axis=0)
        top = jnp.where(row8 < s, pltpu.roll(carry, s, axis=0), y[0:SUBLANES])
        return jnp.concatenate([top, y[SUBLANES:]], axis=0)

    def conv_group(gi, out_ref):
        c0 = gi * A_WIDTH
        scale = A_HEAD_DIM ** -0.5 if gi == 0 else 1.0
        for hd in range(A_HEADS):
            sl = slice(hd * A_HEAD_DIM, (hd + 1) * A_HEAD_DIM)
            wsl = slice(c0 + hd * A_HEAD_DIM, c0 + (hd + 1) * A_HEAD_DIM)
            w0, w1, w2, w3 = [cw_ref[j:j + 1, wsl] for j in range(CONV_K)]
            hc = scr_ref[:, wsl]
            h1 = shift_rows(hc, prev_ref[gi, :, sl], 1)
            r = w1 * hc + w0 * h1
            acc = w3 * hc + w2 * h1 + shift_rows(r, prevr_ref[gi, :, sl], 2)
            prev_ref[gi, :, sl] = hc[tm - SUBLANES:tm]
            prevr_ref[gi, :, sl] = r[tm - SUBLANES:tm]
            a = _silu(acc)
            if gi < 2:
                ss = jnp.sum(a * a, axis=-1, keepdims=True)
                a = a * (lax.rsqrt(ss + L2_EPS) * scale)
            out_ref[0, :, sl] = a.astype(out_ref.dtype)

    conv_group(0, qa_ref)
    conv_group(1, ka_ref)
    conv_group(2, va_ref)
    za_ref[0] = scr_ref[:, 3 * A_WIDTH:4 * A_WIDTH].astype(za_ref.dtype)
    qbt_ref[0] = (scrt_ref[:B_WIDTH, :] * (B_HEAD_DIM ** -0.5 * LOG2E)).astype(qbt_ref.dtype)
    vbt_ref[0] = scrt_ref[B_WIDTH:, :].astype(vbt_ref.dtype)
    kb_ref[0] = scr_ref[:, COLS_A + COL_KB:COLS_A + COL_KB + B_KV_WIDTH].astype(kb_ref.dtype)
    zb_ref[0] = scr_ref[:, COLS_A + COL_ZB:COLS_A + COL_ZB + B_WIDTH].astype(zb_ref.dtype)
    hs = scr_ref[:, COLS_A + COL_SMALL:COLS_A + COL_SMALL + LANES]

    lane = lax.broadcasted_iota(jnp.int32, hs.shape, 1)
    beta = _sigmoid(hs)
    g = -jnp.exp(alog_ref[...]) * _softplus(hs + dtb_ref[...])
    bg_ref[0] = jnp.where(lane < A_HEADS, beta, g)


def _inproj(x, w_in_all, layer, wb, wt, cw, alog_vec, dtb_vec):
    bsz, t_len, d = x.shape
    tm = TM_PROJ
    grid = (bsz, t_len // tm)
    tok = lambda width: pl.BlockSpec((1, tm, width), lambda b, t: (b, t, 0))
    tok_t = lambda rows: pl.BlockSpec((1, rows, tm), lambda b, t: (b, 0, t))
    full = lambda a: pl.BlockSpec(a.shape, lambda b, t: (0,) * a.ndim)
    wa_spec = pl.BlockSpec((None, d, COLS_A), lambda b, t: (layer, 0, 0))
    out_shapes = [
        jax.ShapeDtypeStruct((bsz, t_len, A_WIDTH), BF16),
        jax.ShapeDtypeStruct((bsz, t_len, A_WIDTH), BF16),
        jax.ShapeDtypeStruct((bsz, t_len, A_WIDTH), BF16),
        jax.ShapeDtypeStruct((bsz, t_len, A_WIDTH), BF16),
        jax.ShapeDtypeStruct((bsz, t_len, LANES), F32),
        jax.ShapeDtypeStruct((bsz, B_WIDTH, t_len), BF16),
        jax.ShapeDtypeStruct((bsz, t_len, B_KV_WIDTH), BF16),
        jax.ShapeDtypeStruct((bsz, B_KV_WIDTH, t_len), BF16),
        jax.ShapeDtypeStruct((bsz, t_len, B_WIDTH), BF16),
    ]
    out_specs = [tok(A_WIDTH), tok(A_WIDTH), tok(A_WIDTH), tok(A_WIDTH), tok(LANES),
                 tok_t(B_WIDTH), tok(B_KV_WIDTH), tok_t(B_KV_WIDTH), tok(B_WIDTH)]
    return pl.pallas_call(
        _inproj_kernel,
        grid=grid,
        in_specs=[tok(d), wa_spec, full(wb), full(wt), full(cw), full(alog_vec), full(dtb_vec)],
        out_specs=out_specs,
        out_shape=out_shapes,
        scratch_shapes=[
            pltpu.VMEM((3, SUBLANES, A_WIDTH), F32),
            pltpu.VMEM((3, SUBLANES, A_WIDTH), F32),
            pltpu.VMEM((tm, COLS_A + COLS_B), F32),
            pltpu.VMEM((B_WIDTH + B_KV_WIDTH, tm), F32),
        ],
        compiler_params=pltpu.CompilerParams(
            dimension_semantics=("arbitrary", "arbitrary"),
            vmem_limit_bytes=VMEM_LIMIT_BYTES),
        name="inproj",
    )(x, w_in_all, wb, wt, cw, alog_vec, dtb_vec)


def _split3(x):
    hi = x.astype(BF16)
    r1 = x - hi.astype(F32)
    mid = r1.astype(BF16)
    lo = (r1 - mid.astype(F32)).astype(BF16)
    return hi, mid, lo


def _delta_kernel(q_ref, k_ref, v_ref, bg_ref, nw_ref, o_ref, s_ref):
    tb = q_ref.shape[1]
    n_chunks = tb // CHUNK
    t = pl.program_id(1)

    @pl.when(t == 0)
    def _():
        s_ref[...] = jnp.zeros_like(s_ref)

    bg = bg_ref[0]
    r_i = lax.broadcasted_iota(jnp.int32, (CHUNK, CHUNK), 0)
    c_i = lax.broadcasted_iota(jnp.int32, (CHUNK, CHUNK), 1)
    tril = jnp.where(c_i <= r_i, 1.0, 0.0).astype(BF16)
    bg_cat = jnp.concatenate([bg[c * CHUNK:(c + 1) * CHUNK] for c in range(n_chunks)], axis=1)
    hi, mid, lo = _split3(bg_cat)
    gcum_cat = _dot(tril, hi) + _dot(tril, mid) + _dot(tril, lo)

    row = lax.broadcasted_iota(jnp.int32, (CHUNK, LANES), 0)
    lane = lax.broadcasted_iota(jnp.int32, (CHUNK, LANES), 1)
    col = jnp.where(lane >= CHUNK, lane - CHUNK, lane)
    causal = col <= row
    strict = col < row
    lo_half = lane < CHUNK
    lo_row = lo_half[0:1]
    eye2 = jnp.where(col == row, 1.0, 0.0)
    nw = nw_ref[...]
    half = LANES // 2

    csl = lambda c: slice(c * CHUNK, (c + 1) * CHUNK)
    hsl = lambda hd: slice(hd * A_HEAD_DIM, (hd + 1) * A_HEAD_DIM)
    gcums = [gcum_cat[:, c * LANES:(c + 1) * LANES] for c in range(n_chunks)]
    g2ts = [jnp.concatenate([g, g], axis=0).T for g in gcums]
    gidx = lambda hd: slice(A_HEADS + hd, A_HEADS + hd + 1)

    st = {}
    chains = {}

    def stage_products(c, m):
        ha, hb = 2 * m, 2 * m + 1
        gcum = gcums[c]
        qs = [q_ref[0, csl(c), hsl(h)] for h in (ha, hb)]
        ks = [k_ref[0, csl(c), hsl(h)] for h in (ha, hb)]
        gcols = [gcum[:, gidx(h)] for h in (ha, hb)]
        bcols = [bg[csl(c), h:h + 1] for h in (ha, hb)]
        glasts = [gcum[CHUNK - 1:CHUNK, gidx(h)] for h in (ha, hb)]
        gcol_p = jnp.where(lo_half, gcols[0], gcols[1])
        grow_p = jnp.where(lo_row, g2ts[c][gidx(ha), :], g2ts[c][gidx(hb), :])
        bcol_p = jnp.where(lo_half, bcols[0], bcols[1])
        decay = jnp.where(causal, jnp.exp(jnp.minimum(gcol_p - grow_p, 0.0)), 0.0)
        kt = jnp.concatenate(ks, axis=0).astype(F32).T
        prod = _dot(jnp.concatenate([qs[0], ks[0], qs[1], ks[1]], axis=0), kt.astype(BF16))
        qk = jnp.where(lo_half, prod[0:CHUNK], prod[2 * CHUNK:3 * CHUNK]) * decay
        kk = jnp.where(lo_half, prod[CHUNK:2 * CHUNK], prod[3 * CHUNK:])
        glast_row = jnp.where(lo_row, glasts[0], glasts[1])
        st[(c, m)] = dict(qs=qs, ks=ks, gcols=gcols, bcols=bcols, glasts=glasts, qk=qk,
                          kdt=kt * jnp.exp(glast_row - grow_p),
                          t=eye2, p=jnp.where(strict, -(kk * decay * bcol_p), 0.0))

    n_levels = 6

    def stage_double(pr, lvl):
        d = st[pr]
        pb = d["p"].astype(BF16)
        tb16 = d["t"].astype(BF16)
        zero = jnp.zeros_like(pb)
        if lvl < n_levels - 1:
            rhs = jnp.concatenate(
                [jnp.concatenate([jnp.where(lo_half, tb16, zero), jnp.where(lo_half, pb, zero)], axis=1),
                 jnp.concatenate([jnp.where(lo_half, zero, tb16), jnp.where(lo_half, zero, pb)], axis=1)],
                axis=0)
            r = _dot(pb, rhs)
            d["t"] = d["t"] + r[:, :LANES]
            d["p"] = r[:, LANES:]
        else:
            rhs = jnp.concatenate([jnp.where(lo_half, tb16, zero), jnp.where(lo_half, zero, tb16)], axis=0)
            d["t"] = d["t"] + _dot(pb, rhs)

    def stage_uw(c, m):
        d = st[(c, m)]
        t_sw = pltpu.roll(d["t"], half, axis=1)
        kdt_sw = pltpu.roll(d["kdt"], half, axis=1)
        qk_sw = pltpu.roll(d["qk"], half, axis=1)
        for i, (t_x, kdt_x, qk_x) in enumerate(((d["t"], d["kdt"], d["qk"]), (t_sw, kdt_sw, qk_sw))):
            hd = 2 * m + i
            kf = d["ks"][i].astype(F32)
            vf = v_ref[0, csl(c), hsl(hd)].astype(F32)
            bcol, gcol, glast = d["bcols"][i], d["gcols"][i], d["glasts"][i]
            eg = jnp.exp(gcol)
            x_cat = jnp.concatenate([vf * bcol, kf * (bcol * eg)], axis=1).astype(BF16)
            uw = _dot(t_x[:, :half].astype(BF16), x_cat)
            lhs = jnp.concatenate([kdt_x[:, :half], qk_x[:, :half]], axis=0).astype(BF16)
            chains[(c, hd)] = dict(uw=uw.astype(BF16), lhs=lhs, eg=eg, glast=glast, q=d["qs"][i])

    def stage_cm(key):
        d = chains[key]
        r = _dot(d["lhs"], d["uw"])
        d["cm"] = r[:A_HEAD_DIM, :A_HEAD_DIM]
        d["mm"] = r[:A_HEAD_DIM, A_HEAD_DIM:]
        d["oi"] = r[A_HEAD_DIM:, :A_HEAD_DIM]
        d["qp"] = d["q"].astype(F32) * d["eg"] - r[A_HEAD_DIM:, A_HEAD_DIM:]

    def stage_seq(c):
        for hd in range(A_HEADS):
            d = chains[(c, hd)]
            s_old = s_ref[hd]
            lhs_seq = jnp.concatenate([d["mm"], d["qp"]], axis=0).astype(BF16)
            r_seq = _dot(lhs_seq, s_old.astype(BF16))
            s_ref[hd] = jnp.exp(d["glast"]) * s_old + d["cm"] - r_seq[:A_HEAD_DIM]
            o = r_seq[A_HEAD_DIM:] + d["oi"]
            ms = jnp.mean(o * o, axis=-1, keepdims=True)
            o_ref[0, csl(c), hsl(hd)] = (o * lax.rsqrt(ms + RMS_EPS) * nw).astype(o_ref.dtype)

    group = n_chunks // N_GROUPS
    group_slots = []
    for g0 in range(0, n_chunks, group):
        gp = [(c, m) for c in range(g0, g0 + group) for m in range(A_HEADS // 2)]
        slots = [lambda gp=gp: [stage_products(c, m) for (c, m) in gp]]
        for lvl in range(n_levels):
            slots.append(lambda gp=gp, lvl=lvl: [stage_double(pr, lvl) for pr in gp])
        slots.append(lambda gp=gp: [stage_uw(c, m) for (c, m) in gp])
        slots.append(lambda gp=gp: [stage_cm((c, 2 * m + i)) for (c, m) in gp for i in range(2)])
        for c in range(g0, g0 + group):
            slots.append(lambda c=c: stage_seq(c))
        group_slots.append(slots)
    n_slots = len(group_slots[0])
    for tick in range(n_slots + SKEW * (N_GROUPS - 1)):
        for g, slots in enumerate(group_slots):
            k = tick - g * SKEW
            if 0 <= k < n_slots:
                slots[k]()


def _delta(qa, ka, va, bg, nw):
    bsz, t_len, _ = qa.shape
    tb = TB_DELTA
    grid = (bsz, t_len // tb)
    tok = lambda width: pl.BlockSpec((1, tb, width), lambda b, t: (b, t, 0))
    return pl.pallas_call(
        _delta_kernel,
        grid=grid,
        in_specs=[tok(A_WIDTH), tok(A_WIDTH), tok(A_WIDTH), tok(LANES),
                  pl.BlockSpec(nw.shape, lambda b, t: (0, 0))],
        out_specs=tok(A_WIDTH),
        out_shape=jax.ShapeDtypeStruct((bsz, t_len, A_WIDTH), BF16),
        scratch_shapes=[pltpu.VMEM((A_HEADS, A_HEAD_DIM, A_HEAD_DIM), F32)],
        compiler_params=pltpu.CompilerParams(
            dimension_semantics=("arbitrary", "arbitrary"),
            vmem_limit_bytes=VMEM_LIMIT_BYTES),
        name="delta",
    )(qa, ka, va, bg, nw)


def _swa_kernel(qt_ref, kp_ref, kc_ref, vtp_ref, vtc_ref, slope_ref, sink_ref, o_ref):
    n = pl.program_id(1)
    nqb = qt_ref.shape[2] // BLOCK
    band = 2 * BLOCK
    hq = B_Q_HEADS * BLOCK
    k_all = jnp.concatenate([kp_ref[0], kc_ref[0]], axis=0)
    vt_all = jnp.concatenate([vtp_ref[0], vtc_ref[0]], axis=1)

    s_idx = lax.broadcasted_iota(jnp.int32, (band, BLOCK), 0)
    q_idx = lax.broadcasted_iota(jnp.int32, (band, BLOCK), 1)
    dist = q_idx + BLOCK - s_idx
    valid = (dist >= 0) & (dist < WINDOW)
    dist_f = dist.astype(F32)
    hconst = lambda ref, h, shape: jnp.broadcast_to(ref[h:h + 1, 0:1], shape)
    bias = [jnp.where(valid, -(hconst(slope_ref, h, (band, BLOCK)) * LOG2E) * dist_f, -jnp.inf)
            for h in range(B_Q_HEADS)]
    sink = [hconst(sink_ref, h, (1, BLOCK)) * LOG2E for h in range(B_Q_HEADS)]
    no_prev = s_idx < BLOCK
    zq = jnp.zeros((B_HEAD_DIM, B_GROUP * BLOCK), BF16)

    scores = []
    for i in range(nqb):
        qs = slice(i * BLOCK, (i + 1) * BLOCK)
        heads = [qt_ref[0, h * B_HEAD_DIM:(h + 1) * B_HEAD_DIM, qs] for h in range(B_Q_HEADS)]
        q_rhs = jnp.concatenate(
            [jnp.concatenate(heads[:B_GROUP] + [zq], axis=1),
             jnp.concatenate([zq] + heads[B_GROUP:], axis=1)], axis=0)
        scores.append(_dot(k_all[i * BLOCK:i * BLOCK + band], q_rhs))

    probs, invs = [], []
    for i in range(nqb):
        p_heads, inv_heads = [], []
        for h in range(B_Q_HEADS):
            s = scores[i][:, h * BLOCK:(h + 1) * BLOCK] + bias[h]
            if i == 0:
                s = jnp.where(no_prev & (n == 0), -jnp.inf, s)
            m = jnp.maximum(jnp.max(s, axis=0, keepdims=True), sink[h])
            p = jnp.exp2(s - m)
            den = jnp.sum(p, axis=0, keepdims=True) + jnp.exp2(sink[h] - m)
            p_heads.append(p.astype(BF16))
            inv_heads.append(1.0 / den)
        probs.append(jnp.concatenate(p_heads, axis=1))
        invs.append(inv_heads)

    outs = [_dot(vt_all[:, i * BLOCK:i * BLOCK + band], probs[i]) for i in range(nqb)]

    for i in range(nqb):
        qs = slice(i * BLOCK, (i + 1) * BLOCK)
        for j in range(B_Q_HEADS // 2):
            r0 = (2 * j // B_GROUP) * B_HEAD_DIM
            pair_t = jnp.concatenate(
                [outs[i][r0:r0 + B_HEAD_DIM, (2 * j + c) * BLOCK:(2 * j + c + 1) * BLOCK] * invs[i][2 * j + c]
                 for c in range(2)], axis=0)
            o_ref[0, qs, j * LANES:(j + 1) * LANES] = pair_t.T.astype(o_ref.dtype)


def _swa(qbt, kb, vbt, slope_rows, sink_rows):
    bsz, _, t_len = qbt.shape
    nqb = QB_SWA
    tq = nqb * BLOCK
    grid = (bsz, t_len // tq)
    prev_blk = lambda n: jnp.maximum(n * nqb - 1, 0)
    full = lambda a: pl.BlockSpec(a.shape, lambda b, n: (0,) * a.ndim)
    return pl.pallas_call(
        _swa_kernel,
        grid=grid,
        in_specs=[pl.BlockSpec((1, B_WIDTH, tq), lambda b, n: (b, 0, n)),
                  pl.BlockSpec((1, BLOCK, B_KV_WIDTH), lambda b, n: (b, prev_blk(n), 0)),
                  pl.BlockSpec((1, tq, B_KV_WIDTH), lambda b, n: (b, n, 0)),
                  pl.BlockSpec((1, B_KV_WIDTH, BLOCK), lambda b, n: (b, 0, prev_blk(n))),
                  pl.BlockSpec((1, B_KV_WIDTH, tq), lambda b, n: (b, 0, n)),
                  full(slope_rows), full(sink_rows)],
        out_specs=pl.BlockSpec((1, tq, B_WIDTH), lambda b, n: (b, n, 0)),
        out_shape=jax.ShapeDtypeStruct((bsz, t_len, B_WIDTH), BF16),
        compiler_params=pltpu.CompilerParams(
            dimension_semantics=("arbitrary", "arbitrary"),
            vmem_limit_bytes=VMEM_LIMIT_BYTES),
        name="swa",
    )(qbt, kb, kb, vbt, vbt, slope_rows, sink_rows)


def _outproj_kernel(alpha, x_ref, ya_ref, za_ref, yb_ref, zb_ref, w_ref, g_ref, b_ref, o_ref):
    ya = (ya_ref[0].astype(F32) * _silu(za_ref[0].astype(F32))).astype(BF16)
    yb = (yb_ref[0].astype(F32) * _silu(zb_ref[0].astype(F32))).astype(BF16)
    y = _dot(ya, w_ref[:A_WIDTH, :]) + _dot(yb, w_ref[A_WIDTH:, :])
    r = alpha * x_ref[0] + y
    mu = jnp.mean(r, axis=-1, keepdims=True)
    rc = r - mu
    var = jnp.mean(rc * rc, axis=-1, keepdims=True)
    o_ref[0] = rc * lax.rsqrt(var + LN_EPS) * g_ref[...] + b_ref[...]


def _outproj(x, ya, za, yb, zb, w_out_all, layer, ln_g, ln_b, alpha):
    bsz, t_len, d = x.shape
    tm = TM_OUT
    grid = (bsz, t_len // tm)
    tok = lambda width: pl.BlockSpec((1, tm, width), lambda b, t: (b, t, 0))
    full = lambda a: pl.BlockSpec(a.shape, lambda b, t: (0,) * a.ndim)
    w_spec = pl.BlockSpec((None,) + w_out_all.shape[1:], lambda b, t: (layer, 0, 0))
    return pl.pallas_call(
        functools.partial(_outproj_kernel, alpha),
        grid=grid,
        in_specs=[tok(d), tok(A_WIDTH), tok(A_WIDTH), tok(B_WIDTH), tok(B_WIDTH), w_spec,
                  full(ln_g), full(ln_b)],
        out_specs=tok(d),
        out_shape=jax.ShapeDtypeStruct((bsz, t_len, d), F32),
        compiler_params=pltpu.CompilerParams(
            dimension_semantics=("arbitrary", "arbitrary"),
            vmem_limit_bytes=VMEM_LIMIT_BYTES),
        name="outproj",
    )(x, ya, za, yb, zb, w_out_all, ln_g, ln_b)


def _transpose_kernel(w_ref, o_ref):
    o_ref[0] = w_ref[0].astype(F32).T.astype(o_ref.dtype)


def _transpose_weights(w):
    depth, d, n = w.shape
    return pl.pallas_call(
        _transpose_kernel,
        grid=(depth,),
        in_specs=[pl.BlockSpec((1, d, n), lambda l: (l, 0, 0))],
        out_specs=pl.BlockSpec((1, n, d), lambda l: (l, 0, 0)),
        out_shape=jax.ShapeDtypeStruct((depth, n, d), BF16),
        compiler_params=pltpu.CompilerParams(
            dimension_semantics=("arbitrary",), vmem_limit_bytes=VMEM_LIMIT_BYTES),
        name="wtranspose",
    )(w)


def _lane_rows(vals):
    return jnp.broadcast_to(vals.astype(F32)[:, None], (vals.shape[0], LANES))


def kernel(x, w_in, conv_w, a_log, dt_bias, norm_w, sinks, w_out, ln_g, ln_b):
    depth, d, _ = w_in.shape
    alpha = (2 * depth) ** 0.25
    w_in_bf = w_in.astype(BF16)
    w_out_bf = w_out.astype(BF16)
    o_small = COLS_A
    o_qb = o_small + 2 * A_HEADS
    o_kb = o_qb + B_WIDTH
    o_vb = o_kb + B_KV_WIDTH
    o_zb = o_vb + B_KV_WIDTH
    wb_all = jnp.concatenate(
        [w_in_bf[:, :, o_kb:o_vb], w_in_bf[:, :, o_zb:], w_in_bf[:, :, o_small:o_qb],
         jnp.zeros((depth, d, LANES - 2 * A_HEADS), BF16)], axis=2)
    wt_all = _transpose_weights(
        jnp.concatenate([w_in_bf[:, :, o_qb:o_kb], w_in_bf[:, :, o_vb:o_zb]], axis=2))
    pad = jnp.zeros((LANES - 2 * A_HEADS,), F32)
    zeros_h = jnp.zeros((A_HEADS,), F32)
    slope_rows = _lane_rows(
        jnp.asarray([2.0 ** (-8.0 * (h + 1) / B_Q_HEADS) for h in range(B_Q_HEADS)], F32))
    for l in range(depth):
        alog_vec = jnp.concatenate([zeros_h, a_log[l].astype(F32), pad])[None, :]
        dtb_vec = jnp.concatenate([zeros_h, dt_bias[l].astype(F32), pad])[None, :]
        qa, ka, va, za, bg, qbt, kb, vbt, zb = _inproj(
            x, w_in_bf, l, wb_all[l], wt_all[l], conv_w[l].astype(F32), alog_vec, dtb_vec)
        ya = _delta(qa, ka, va, bg, norm_w[l].astype(F32)[None, :])
        yb = _swa(qbt, kb, vbt, slope_rows, _lane_rows(sinks[l]))
        x = _outproj(x, ya, za, yb, zb, w_out_bf, l, ln_g[l].astype(F32)[None, :],
                     ln_b[l].astype(F32)[None, :], alpha)
    return x
```

```python
import functools

import jax
import jax.numpy as jnp
from jax import lax
from jax.experimental import pallas as pl
from jax.experimental.pallas import tpu as pltpu

F32 = jnp.float32
BF16 = jnp.bfloat16

LANES = 128
SUBLANES = 8
VMEM_LIMIT_BYTES = 48 * 1024 * 1024

A_HEADS = 4
A_HEAD_DIM = 128
A_WIDTH = A_HEADS * A_HEAD_DIM
CONV_K = 4
CHUNK = 64
B_Q_HEADS = 8
B_KV_HEADS = 2
B_HEAD_DIM = 64
B_GROUP = B_Q_HEADS // B_KV_HEADS
B_WIDTH = B_Q_HEADS * B_HEAD_DIM
B_KV_WIDTH = B_KV_HEADS * B_HEAD_DIM
WINDOW = 128
BLOCK = 128
LN_EPS = 1e-5
RMS_EPS = 1e-6
L2_EPS = 1e-6
LOG2E = 1.4426950408889634

COLS_A = 4 * A_WIDTH
COL_KB = 0
COL_ZB = COL_KB + B_KV_WIDTH
COL_SMALL = COL_ZB + B_WIDTH
COLS_B = COL_SMALL + LANES

TM_PROJ = 512
PROJ_COLS = 256
TM_OUT = 1024
OUT_BUFFERS = 3
TB_DELTA = 2048
N_GROUPS = 32
SKEW = 1
QB_SWA = 8


def _silu(x):
    hx = 0.5 * x
    return hx * jnp.tanh(hx) + hx


def _sigmoid(x):
    return 1.0 / (1.0 + jnp.exp(-x))


def _softplus(x):
    return jnp.maximum(x, 0.0) + jnp.log(1.0 + jnp.exp(-jnp.abs(x)))


def _dot(a, b):
    return jnp.dot(a, b, preferred_element_type=F32)


def _dot_nt(a, b):
    return lax.dot_general(a, b, (((1,), (1,)), ((), ())), preferred_element_type=F32)


def _inproj_kernel(x_ref, wa_ref, wb_ref, wt_ref, cw_ref, alog_ref, dtb_ref,
                   qa_ref, ka_ref, va_ref, za_ref, bg_ref, qbt_ref, kb_ref, vbt_ref, zb_ref,
                   prev_ref, prevr_ref, scr_ref, scrt_ref):
    tm = x_ref.shape[1]
    t = pl.program_id(1)

    @pl.when(t == 0)
    def _():
        prev_ref[...] = jnp.zeros_like(prev_ref)
        prevr_ref[...] = jnp.zeros_like(prevr_ref)

    xb = x_ref[0].astype(BF16)

    for c0 in range(0, COLS_A, PROJ_COLS):
        scr_ref[:, c0:c0 + PROJ_COLS] = _dot(xb, wa_ref[:, c0:c0 + PROJ_COLS])
    scrt_ref[...] = _dot_nt(wt_ref[...], xb)
    for c0 in range(0, COLS_B, PROJ_COLS):
        scr_ref[:, COLS_A + c0:COLS_A + c0 + PROJ_COLS] = _dot(xb, wb_ref[:, c0:c0 + PROJ_COLS])

    row8 = lax.broadcasted_iota(jnp.int32, (SUBLANES, A_HEAD_DIM), 0)

    def shift_rows(x, carry, s):
        y = pltpu.roll(x, s, axis=0)
        top = jnp.where(row8 < s, pltpu.roll(carry, s, axis=0), y[0:SUBLANES])
        return jnp.concatenate([top, y[SUBLANES:]], axis=0)

    def conv_group(gi, out_ref):
        c0 = gi * A_WIDTH
        scale = A_HEAD_DIM ** -0.5 if gi == 0 else 1.0
        for hd in range(A_HEADS):
            sl = slice(hd * A_HEAD_DIM, (hd + 1) * A_HEAD_DIM)
            wsl = slice(c0 + hd * A_HEAD_DIM, c0 + (hd + 1) * A_HEAD_DIM)
            w0, w1, w2, w3 = [cw_ref[j:j + 1, wsl] for j in range(CONV_K)]
            hc = scr_ref[:, wsl]
            h1 = shift_rows(hc, prev_ref[gi, :, sl], 1)
            r = w1 * hc + w0 * h1
            acc = w3 * hc + w2 * h1 + shift_rows(r, prevr_ref[gi, :, sl], 2)
            prev_ref[gi, :, sl] = hc[tm - SUBLANES:tm]
            prevr_ref[gi, :, sl] = r[tm - SUBLANES:tm]
            a = _silu(acc)
            if gi < 2:
                ss = jnp.sum(a * a, axis=-1, keepdims=True)
                a = a * (lax.rsqrt(ss + L2_EPS) * scale)
            out_ref[0, :, sl] = a.astype(out_ref.dtype)

    conv_group(0, qa_ref)
    conv_group(1, ka_ref)
    conv_group(2, va_ref)
    za_ref[0] = scr_ref[:, 3 * A_WIDTH:4 * A_WIDTH].astype(za_ref.dtype)
    qbt_ref[0] = (scrt_ref[:B_WIDTH, :] * (B_HEAD_DIM ** -0.5 * LOG2E)).astype(qbt_ref.dtype)
    vbt_ref[0] = scrt_ref[B_WIDTH:, :].astype(vbt_ref.dtype)
    kb_ref[0] = scr_ref[:, COLS_A + COL_KB:COLS_A + COL_KB + B_KV_WIDTH].astype(kb_ref.dtype)
    zb_ref[0] = scr_ref[:, COLS_A + COL_ZB:COLS_A + COL_ZB + B_WIDTH].astype(zb_ref.dtype)
    hs = scr_ref[:, COLS_A + COL_SMALL:COLS_A + COL_SMALL + LANES]

    lane = lax.broadcasted_iota(jnp.int32, hs.shape, 1)
    beta = _sigmoid(hs)
    g = -jnp.exp(alog_ref[...]) * _softplus(hs + dtb_ref[...])
    bg_ref[0] = jnp.where(lane < A_HEADS, beta, g)


def _inproj(x, w_in_all, layer, wb, wt, cw, alog_vec, dtb_vec):
    bsz, t_len, d = x.shape
    tm = TM_PROJ
    grid = (bsz, t_len // tm)
    tok = lambda width: pl.BlockSpec((1, tm, width), lambda b, t: (b, t, 0))
    tok_t = lambda rows: pl.BlockSpec((1, rows, tm), lambda b, t: (b, 0, t))
    full = lambda a: pl.BlockSpec(a.shape, lambda b, t: (0,) * a.ndim)
    wa_spec = pl.BlockSpec((None, d, COLS_A), lambda b, t: (layer, 0, 0))
    out_shapes = [
        jax.ShapeDtypeStruct((bsz, t_len, A_WIDTH), BF16),
        jax.ShapeDtypeStruct((bsz, t_len, A_WIDTH), BF16),
        jax.ShapeDtypeStruct((bsz, t_len, A_WIDTH), BF16),
        jax.ShapeDtypeStruct((bsz, t_len, A_WIDTH), BF16),
        jax.ShapeDtypeStruct((bsz, t_len, LANES), F32),
        jax.ShapeDtypeStruct((bsz, B_WIDTH, t_len), BF16),
        jax.ShapeDtypeStruct((bsz, t_len, B_KV_WIDTH), BF16),
        jax.ShapeDtypeStruct((bsz, B_KV_WIDTH, t_len), BF16),
        jax.ShapeDtypeStruct((bsz, t_len, B_WIDTH), BF16),
    ]
    out_specs = [tok(A_WIDTH), tok(A_WIDTH), tok(A_WIDTH), tok(A_WIDTH), tok(LANES),
                 tok_t(B_WIDTH), tok(B_KV_WIDTH), tok_t(B_KV_WIDTH), tok(B_WIDTH)]
    return pl.pallas_call(
        _inproj_kernel,
        grid=grid,
        in_specs=[tok(d), wa_spec, full(wb), full(wt), full(cw), full(alog_vec), full(dtb_vec)],
        out_specs=out_specs,
        out_shape=out_shapes,
        scratch_shapes=[
            pltpu.VMEM((3, SUBLANES, A_WIDTH), F32),
            pltpu.VMEM((3, SUBLANES, A_WIDTH), F32),
            pltpu.VMEM((tm, COLS_A + COLS_B), F32),
            pltpu.VMEM((B_WIDTH + B_KV_WIDTH, tm), F32),
        ],
        compiler_params=pltpu.CompilerParams(
            dimension_semantics=("arbitrary", "arbitrary"),
            vmem_limit_bytes=VMEM_LIMIT_BYTES),
        name="inproj",
    )(x, w_in_all, wb, wt, cw, alog_vec, dtb_vec)


def _split3(x):
    hi = x.astype(BF16)
    r1 = x - hi.astype(F32)
    mid = r1.astype(BF16)
    lo = (r1 - mid.astype(F32)).astype(BF16)
    return hi, mid, lo


def _delta_kernel(q_ref, k_ref, v_ref, bg_ref, nw_ref, o_ref, s_ref):
    tb = q_ref.shape[1]
    n_chunks = tb // CHUNK
    t = pl.program_id(1)

    @pl.when(t == 0)
    def _():
        s_ref[...] = jnp.zeros_like(s_ref)

    bg = bg_ref[0]
    r_i = lax.broadcasted_iota(jnp.int32, (CHUNK, CHUNK), 0)
    c_i = lax.broadcasted_iota(jnp.int32, (CHUNK, CHUNK), 1)
    tril = jnp.where(c_i <= r_i, 1.0, 0.0).astype(BF16)
    bg_cat = jnp.concatenate([bg[c * CHUNK:(c + 1) * CHUNK] for c in range(n_chunks)], axis=1)
    hi, mid, lo = _split3(bg_cat)
    gcum_cat = _dot(tril, hi) + _dot(tril, mid) + _dot(tril, lo)

    row = lax.broadcasted_iota(jnp.int32, (CHUNK, LANES), 0)
    lane = lax.broadcasted_iota(jnp.int32, (CHUNK, LANES), 1)
    col = jnp.where(lane >= CHUNK, lane - CHUNK, lane)
    causal = col <= row
    strict = col < row
    lo_half = lane < CHUNK
    lo_row = lo_half[0:1]
    eye2 = jnp.where(col == row, 1.0, 0.0)
    nw = nw_ref[...]
    half = LANES // 2

    csl = lambda c: slice(c * CHUNK, (c + 1) * CHUNK)
    hsl = lambda hd: slice(hd * A_HEAD_DIM, (hd + 1) * A_HEAD_DIM)
    gcums = [gcum_cat[:, c * LANES:(c + 1) * LANES] for c in range(n_chunks)]
    g2ts = [jnp.concatenate([g, g], axis=0).T for g in gcums]
    gidx = lambda hd: slice(A_HEADS + hd, A_HEADS + hd + 1)

    st = {}
    chains = {}

    def stage_products(c, m):
        ha, hb = 2 * m, 2 * m + 1
        gcum = gcums[c]
        qs = [q_ref[0, csl(c), hsl(h)] for h in (ha, hb)]
        ks = [k_ref[0, csl(c), hsl(h)] for h in (ha, hb)]
        gcols = [gcum[:, gidx(h)] for h in (ha, hb)]
        bcols = [bg[csl(c), h:h + 1] for h in (ha, hb)]
        glasts = [gcum[CHUNK - 1:CHUNK, gidx(h)] for h in (ha, hb)]
        gcol_p = jnp.where(lo_half, gcols[0], gcols[1])
        grow_p = jnp.where(lo_row, g2ts[c][gidx(ha), :], g2ts[c][gidx(hb), :])
        bcol_p = jnp.where(lo_half, bcols[0], bcols[1])
        decay = jnp.where(causal, jnp.exp(jnp.minimum(gcol_p - grow_p, 0.0)), 0.0)
        kt = jnp.concatenate(ks, axis=0).astype(F32).T
        prod = _dot(jnp.concatenate([qs[0], ks[0], qs[1], ks[1]], axis=0), kt.astype(BF16))
        qk = jnp.where(lo_half, prod[0:CHUNK], prod[2 * CHUNK:3 * CHUNK]) * decay
        kk = jnp.where(lo_half, prod[CHUNK:2 * CHUNK], prod[3 * CHUNK:])
        glast_row = jnp.where(lo_row, glasts[0], glasts[1])
        st[(c, m)] = dict(qs=qs, ks=ks, gcols=gcols, bcols=bcols, glasts=glasts, qk=qk,
                          kdt=kt * jnp.exp(glast_row - grow_p),
                          t=eye2, p=jnp.where(strict, -(kk * decay * bcol_p), 0.0))

    n_levels = 6

    def stage_double(pr, lvl):
        d = st[pr]
        pb = d["p"].astype(BF16)
        tb16 = d["t"].astype(BF16)
        zero = jnp.zeros_like(pb)
        if lvl < n_levels - 1:
            rhs = jnp.concatenate(
                [jnp.concatenate([jnp.where(lo_half, tb16, zero), jnp.where(lo_half, pb, zero)], axis=1),
                 jnp.concatenate([jnp.where(lo_half, zero, tb16), jnp.where(lo_half, zero, pb)], axis=1)],
                axis=0)
            r = _dot(pb, rhs)
            d["t"] = d["t"] + r[:, :LANES]
            d["p"] = r[:, LANES:]
        else:
            rhs = jnp.concatenate([jnp.where(lo_half, tb16, zero), jnp.where(lo_half, zero, tb16)], axis=0)
            d["t"] = d["t"] + _dot(pb, rhs)

    def stage_uw(c, m):
        d = st[(c, m)]
        t_sw = pltpu.roll(d["t"], half, axis=1)
        kdt_sw = pltpu.roll(d["kdt"], half, axis=1)
        qk_sw = pltpu.roll(d["qk"], half, axis=1)
        for i, (t_x, kdt_x, qk_x) in enumerate(((d["t"], d["kdt"], d["qk"]), (t_sw, kdt_sw, qk_sw))):
            hd = 2 * m + i
            kf = d["ks"][i].astype(F32)
            vf = v_ref[0, csl(c), hsl(hd)].astype(F32)
            bcol, gcol, glast = d["bcols"][i], d["gcols"][i], d["glasts"][i]
            eg = jnp.exp(gcol)
            x_cat = jnp.concatenate([vf * bcol, kf * (bcol * eg)], axis=1).astype(BF16)
            uw = _dot(t_x[:, :half].astype(BF16), x_cat)
            lhs = jnp.concatenate([kdt_x[:, :half], qk_x[:, :half]], axis=0).astype(BF16)
            chains[(c, hd)] = dict(uw=uw.astype(BF16), lhs=lhs, eg=eg, glast=glast, q=d["qs"][i])

    def stage_cm(key):
        d = chains[key]
        r = _dot(d["lhs"], d["uw"])
        d["cm"] = r[:A_HEAD_DIM, :A_HEAD_DIM]
        d["mm"] = r[:A_HEAD_DIM, A_HEAD_DIM:]
        d["oi"] = r[A_HEAD_DIM:, :A_HEAD_DIM]
        d["qp"] = d["q"].astype(F32) * d["eg"] - r[A_HEAD_DIM:, A_HEAD_DIM:]

    def stage_seq(c):
        for hd in range(A_HEADS):
            d = chains[(c, hd)]
            s_old = s_ref[hd]
            lhs_seq = jnp.concatenate([d["mm"], d["qp"]], axis=0).astype(BF16)
            r_seq = _dot(lhs_seq, s_old.astype(BF16))
            s_ref[hd] = jnp.exp(d["glast"]) * s_old + d["cm"] - r_seq[:A_HEAD_DIM]
            o = r_seq[A_HEAD_DIM:] + d["oi"]
            ms = jnp.mean(o * o, axis=-1, keepdims=True)
            o_ref[0, csl(c), hsl(hd)] = (o * lax.rsqrt(ms + RMS_EPS) * nw).astype(o_ref.dtype)

    group = n_chunks // N_GROUPS
    group_slots = []
    for g0 in range(0, n_chunks, group):
        gp = [(c, m) for c in range(g0, g0 + group) for m in range(A_HEADS // 2)]
        slots = [lambda gp=gp: [stage_products(c, m) for (c, m) in gp]]
        for lvl in range(n_levels):
            slots.append(lambda gp=gp, lvl=lvl: [stage_double(pr, lvl) for pr in gp])
        slots.append(lambda gp=gp: [stage_uw(c, m) for (c, m) in gp])
        slots.append(lambda gp=gp: [stage_cm((c, 2 * m + i)) for (c, m) in gp for i in range(2)])
        for c in range(g0, g0 + group):
            slots.append(lambda c=c: stage_seq(c))
        group_slots.append(slots)
    n_slots = len(group_slots[0])
    for tick in range(n_slots + SKEW * (N_GROUPS - 1)):
        for g, slots in enumerate(group_slots):
            k = tick - g * SKEW
            if 0 <= k < n_slots:
                slots[k]()


def _delta(qa, ka, va, bg, nw):
    bsz, t_len, _ = qa.shape
    tb = TB_DELTA
    grid = (bsz, t_len // tb)
    tok = lambda width: pl.BlockSpec((1, tb, width), lambda b, t: (b, t, 0))
    return pl.pallas_call(
        _delta_kernel,
        grid=grid,
        in_specs=[tok(A_WIDTH), tok(A_WIDTH), tok(A_WIDTH), tok(LANES),
                  pl.BlockSpec(nw.shape, lambda b, t: (0, 0))],
        out_specs=tok(A_WIDTH),
        out_shape=jax.ShapeDtypeStruct((bsz, t_len, A_WIDTH), BF16),
        scratch_shapes=[pltpu.VMEM((A_HEADS, A_HEAD_DIM, A_HEAD_DIM), F32)],
        compiler_params=pltpu.CompilerParams(
            dimension_semantics=("arbitrary", "arbitrary"),
            vmem_limit_bytes=VMEM_LIMIT_BYTES),
        name="delta",
    )(qa, ka, va, bg, nw)


def _swa_kernel(qt_ref, kp_ref, kc_ref, vtp_ref, vtc_ref, slope_ref, sink_ref, o_ref):
    n = pl.program_id(1)
    nqb = qt_ref.shape[2] // BLOCK
    band = 2 * BLOCK
    hq = B_Q_HEADS * BLOCK
    k_all = jnp.concatenate([kp_ref[0], kc_ref[0]], axis=0)
    vt_all = jnp.concatenate([vtp_ref[0], vtc_ref[0]], axis=1)

    s_idx = lax.broadcasted_iota(jnp.int32, (band, BLOCK), 0)
    q_idx = lax.broadcasted_iota(jnp.int32, (band, BLOCK), 1)
    dist = q_idx + BLOCK - s_idx
    valid = (dist >= 0) & (dist < WINDOW)
    dist_f = dist.astype(F32)
    hconst = lambda ref, h, shape: jnp.broadcast_to(ref[h:h + 1, 0:1], shape)
    bias = [jnp.where(valid, -(hconst(slope_ref, h, (band, BLOCK)) * LOG2E) * dist_f, -jnp.inf)
            for h in range(B_Q_HEADS)]
    sink = [hconst(sink_ref, h, (1, BLOCK)) * LOG2E for h in range(B_Q_HEADS)]
    no_prev = s_idx < BLOCK
    zq = jnp.zeros((B_HEAD_DIM, B_GROUP * BLOCK), BF16)

    scores = []
    for i in range(nqb):
        qs = slice(i * BLOCK, (i + 1) * BLOCK)
        heads = [qt_ref[0, h * B_HEAD_DIM:(h + 1) * B_HEAD_DIM, qs] for h in range(B_Q_HEADS)]
        q_rhs = jnp.concatenate(
            [jnp.concatenate(heads[:B_GROUP] + [zq], axis=1),
             jnp.concatenate([zq] + heads[B_GROUP:], axis=1)], axis=0)
        scores.append(_dot(k_all[i * BLOCK:i * BLOCK + band], q_rhs))

    probs, invs = [], []
    for i in range(nqb):
        p_heads, inv_heads = [], []
        for h in range(B_Q_HEADS):
            s = scores[i][:, h * BLOCK:(h + 1) * BLOCK] + bias[h]
            if i == 0:
                s = jnp.where(no_prev & (n == 0), -jnp.inf, s)
            m = jnp.maximum(jnp.max(s, axis=0, keepdims=True), sink[h])
            p = jnp.exp2(s - m)
            den = jnp.sum(p, axis=0, keepdims=True) + jnp.exp2(sink[h] - m)
            p_heads.append(p.astype(BF16))
            inv_heads.append(1.0 / den)
        probs.append(jnp.concatenate(p_heads, axis=1))
        invs.append(inv_heads)

    outs = [_dot(vt_all[:, i * BLOCK:i * BLOCK + band], probs[i]) for i in range(nqb)]

    for i in range(nqb):
        qs = slice(i * BLOCK, (i + 1) * BLOCK)
        for j in range(B_Q_HEADS // 2):
            r0 = (2 * j // B_GROUP) * B_HEAD_DIM
            pair_t = jnp.concatenate(
                [outs[i][r0:r0 + B_HEAD_DIM, (2 * j + c) * BLOCK:(2 * j + c + 1) * BLOCK] * invs[i][2 * j + c]
                 for c in range(2)], axis=0)
            o_ref[0, qs, j * LANES:(j + 1) * LANES] = pair_t.T.astype(o_ref.dtype)


def _swa(qbt, kb, vbt, slope_rows, sink_rows):
    bsz, _, t_len = qbt.shape
    nqb = QB_SWA
    tq = nqb * BLOCK
    grid = (bsz, t_len // tq)
    prev_blk = lambda n: jnp.maximum(n * nqb - 1, 0)
    full = lambda a: pl.BlockSpec(a.shape, lambda b, n: (0,) * a.ndim)
    return pl.pallas_call(
        _swa_kernel,
        grid=grid,
        in_specs=[pl.BlockSpec((1, B_WIDTH, tq), lambda b, n: (b, 0, n)),
                  pl.BlockSpec((1, BLOCK, B_KV_WIDTH), lambda b, n: (b, prev_blk(n), 0)),
                  pl.BlockSpec((1, tq, B_KV_WIDTH), lambda b, n: (b, n, 0)),
                  pl.BlockSpec((1, B_KV_WIDTH, BLOCK), lambda b, n: (b, 0, prev_blk(n))),
                  pl.BlockSpec((1, B_KV_WIDTH, tq), lambda b, n: (b, 0, n)),
                  full(slope_rows), full(sink_rows)],
        out_specs=pl.BlockSpec((1, tq, B_WIDTH), lambda b, n: (b, n, 0)),
        out_shape=jax.ShapeDtypeStruct((bsz, t_len, B_WIDTH), BF16),
        compiler_params=pltpu.CompilerParams(
            dimension_semantics=("arbitrary", "arbitrary"),
            vmem_limit_bytes=VMEM_LIMIT_BYTES),
        name="swa",
    )(qbt, kb, kb, vbt, vbt, slope_rows, sink_rows)


def _outproj_kernel(alpha, x_ref, ya_ref, za_ref, yb_ref, zb_ref, w_ref, g_ref, b_ref, o_ref):
    ya = (ya_ref[0].astype(F32) * _silu(za_ref[0].astype(F32))).astype(BF16)
    yb = (yb_ref[0].astype(F32) * _silu(zb_ref[0].astype(F32))).astype(BF16)
    y = _dot(ya, w_ref[:A_WIDTH, :]) + _dot(yb, w_ref[A_WIDTH:, :])
    r = alpha * x_ref[0] + y
    mu = jnp.mean(r, axis=-1, keepdims=True)
    rc = r - mu
    var = jnp.mean(rc * rc, axis=-1, keepdims=True)
    o_ref[0] = rc * lax.rsqrt(var + LN_EPS) * g_ref[...] + b_ref[...]


def _outproj(x, ya, za, yb, zb, w_out_all, layer, ln_g, ln_b, alpha):
    bsz, t_len, d = x.shape
    tm = TM_OUT
    grid = (bsz, t_len // tm)
    tok_in = lambda width: pl.BlockSpec((1, tm, width), lambda b, t: (b, t, 0),
                                        pipeline_mode=pl.Buffered(OUT_BUFFERS))
    tok_out = pl.BlockSpec((1, tm, d), lambda b, t: (b, t, 0))

    def outer(x_hbm, ya_hbm, za_hbm, yb_hbm, zb_hbm, w_ref, g_ref, b_ref, o_hbm):
        def body(x_ref, ya_ref, za_ref, yb_ref, zb_ref, o_ref):
            _outproj_kernel(alpha, x_ref, ya_ref, za_ref, yb_ref, zb_ref, w_ref, g_ref, b_ref, o_ref)

        pltpu.emit_pipeline(
            body, grid=grid,
            in_specs=[tok_in(d), tok_in(A_WIDTH), tok_in(A_WIDTH), tok_in(B_WIDTH), tok_in(B_WIDTH)],
            out_specs=[tok_out],
        )(x_hbm, ya_hbm, za_hbm, yb_hbm, zb_hbm, o_hbm)

    any_spec = pl.BlockSpec(memory_space=pl.ANY)
    vmem = lambda a: pl.BlockSpec(a.shape, lambda: (0,) * a.ndim)
    w = w_out_all[layer]
    return pl.pallas_call(
        outer,
        in_specs=[any_spec] * 5 + [vmem(w), vmem(ln_g), vmem(ln_b)],
        out_specs=any_spec,
        out_shape=jax.ShapeDtypeStruct((bsz, t_len, d), F32),
        compiler_params=pltpu.CompilerParams(vmem_limit_bytes=VMEM_LIMIT_BYTES),
        name="outproj",
    )(x, ya, za, yb, zb, w, ln_g, ln_b)


def _transpose_kernel(w_ref, o_ref):
    o_ref[0] = w_ref[0].astype(F32).T.astype(o_ref.dtype)


def _transpose_weights(w):
    depth, d, n = w.shape
    return pl.pallas_call(
        _transpose_kernel,
        grid=(depth,),
        in_specs=[pl.BlockSpec((1, d, n), lambda l: (l, 0, 0))],
        out_specs=pl.BlockSpec((1, n, d), lambda l: (l, 0, 0)),
        out_shape=jax.ShapeDtypeStruct((depth, n, d), BF16),
        compiler_params=pltpu.CompilerParams(
            dimension_semantics=("arbitrary",), vmem_limit_bytes=VMEM_LIMIT_BYTES),
        name="wtranspose",
    )(w)


def _lane_rows(vals):
    return jnp.broadcast_to(vals.astype(F32)[:, None], (vals.shape[0], LANES))


def kernel(x, w_in, conv_w, a_log, dt_bias, norm_w, sinks, w_out, ln_g, ln_b):
    depth, d, _ = w_in.shape
    alpha = (2 * depth) ** 0.25
    w_in_bf = w_in.astype(BF16)
    w_out_bf = w_out.astype(BF16)
    o_small = COLS_A
    o_qb = o_small + 2 * A_HEADS
    o_kb = o_qb + B_WIDTH
    o_vb = o_kb + B_KV_WIDTH
    o_zb = o_vb + B_KV_WIDTH
    wb_all = jnp.concatenate(
        [w_in_bf[:, :, o_kb:o_vb], w_in_bf[:, :, o_zb:], w_in_bf[:, :, o_small:o_qb],
         jnp.zeros((depth, d, LANES - 2 * A_HEADS), BF16)], axis=2)
    wt_all = _transpose_weights(
        jnp.concatenate([w_in_bf[:, :, o_qb:o_kb], w_in_bf[:, :, o_vb:o_zb]], axis=2))
    pad = jnp.zeros((LANES - 2 * A_HEADS,), F32)
    zeros_h = jnp.zeros((A_HEADS,), F32)
    slope_rows = _lane_rows(
        jnp.asarray([2.0 ** (-8.0 * (h + 1) / B_Q_HEADS) for h in range(B_Q_HEADS)], F32))
    for l in range(depth):
        alog_vec = jnp.concatenate([zeros_h, a_log[l].astype(F32), pad])[None, :]
        dtb_vec = jnp.concatenate([zeros_h, dt_bias[l].astype(F32), pad])[None, :]
        qa, ka, va, za, bg, qbt, kb, vbt, zb = _inproj(
            x, w_in_bf, l, wb_all[l], wt_all[l], conv_w[l].astype(F32), alog_vec, dtb_vec)
        ya = _delta(qa, ka, va, bg, norm_w[l].astype(F32)[None, :])
        yb = _swa(qbt, kb, vbt, slope_rows, _lane_rows(sinks[l]))
        x = _outproj(x, ya, za, yb, zb, w_out_bf, l, ln_g[l].astype(F32)[None, :],
                     ln_b[l].astype(F32)[None, :], alpha)
    return x
```
